```python
import jax, jax.numpy as jnp
from jax import lax
import numpy as np

D_MODEL = 1024
BATCH = 8
SEQ = 2048
DEPTH = 2
DEC_BATCH = 128
DEC_SEQ = 8
PAST_LEN = 16384
PAGE_SIZE = 128

MIX = D_MODEL
G_A = MIX // 4
G_B = MIX // 4
G_C = MIX // 4
G_D = MIX // 4
N_MIXERS = 4
HEAD_DIM = 64
A_HEADS = G_A // HEAD_DIM
CHUNK = 128
POOL_WINDOWS = (2, 4, 8, 16)
POOL_GROUPS = len(POOL_WINDOWS)
POOL_GDIM = G_B // POOL_GROUPS
POOL_BUF = max(POOL_WINDOWS) - 1
SCONV_W = 3
DCONV_W = 31
D_FF = 128 * ((8 * D_MODEL // 3 + 127) // 128)
N_ADA = 9
ALPHA = (2 * DEPTH) ** 0.25
BETA = (8 * DEPTH) ** -0.25
SPLIT_SIZES = (G_A, G_A, G_B, G_C, G_C, G_C, G_D, G_D)
IN_COLS = sum(SPLIT_SIZES)
SPLIT_POINTS = tuple(int(s) for s in np.cumsum(SPLIT_SIZES)[:-1])

kernel_name = "hymba_style_conv_pool_sgu_decoder_step"


def layer_norm(x, g, b, eps=1e-5):
    xf = x.astype(jnp.float32)
    mu = jnp.mean(xf, axis=-1, keepdims=True)
    var = jnp.mean(jnp.square(xf - mu), axis=-1, keepdims=True)
    return ((xf - mu) * lax.rsqrt(var + eps) * g + b).astype(x.dtype)


def rms_norm_groups(x, g, n_groups, eps=1e-6):
    b_, t, c = x.shape
    xf = x.astype(jnp.float32).reshape(b_, t, n_groups, c // n_groups)
    xf = xf * lax.rsqrt(jnp.mean(xf * xf, axis=-1, keepdims=True) + eps)
    return (xf.reshape(b_, t, c) * g).astype(x.dtype)


def modulate(x, shift, scale):
    return x * (1.0 + scale[:, None, :]) + shift[:, None, :]


def swiglu(x, w_gate, w_up, w_down):
    return (jax.nn.silu(x @ w_gate) * (x @ w_up)) @ w_down


def causal_dwconv(x_ext, w):
    c = w.shape[1]
    return lax.conv_general_dilated(x_ext, w[:, None, :], window_strides=(1,), padding='VALID',
                                    dimension_numbers=('NWC', 'WIO', 'NWC'), feature_group_count=c)


def sgu_mixer(u, v, ln_g, ln_b, w_s, b_s):
    b_, t, _ = u.shape
    l = CHUNK if t % CHUNK == 0 else t
    u = jax.nn.gelu(u, approximate=False)
    v = layer_norm(jax.nn.gelu(v, approximate=False), ln_g, ln_b)
    vc = v.reshape(b_, t // l, l, A_HEADS, HEAD_DIM)
    mask = jnp.tril(jnp.ones((l, l), dtype=bool))
    w = jnp.where(mask[None], w_s[:, :l, :l], 0.0)
    mixed = jnp.einsum('hts,bcshd->bcthd', w, vc) + jnp.transpose(b_s[:, :l])[:, :, None]
    y = u * mixed.reshape(b_, t, G_A)
    return y, vc[:, -1].reshape(b_, l, G_A)


def pool_mixer(xb, prefix, start_pos, pool_w, pool_scale):
    b_, t, _ = xb.shape
    ext = jnp.concatenate([prefix, xb], axis=1)
    cs = jnp.cumsum(ext.astype(jnp.float32), axis=1)
    cs = jnp.pad(cs, ((0, 0), (1, 0), (0, 0)))
    pos = start_pos + jnp.arange(t)
    outs = []
    for g, win in enumerate(POOL_WINDOWS):
        sl = slice(g * POOL_GDIM, (g + 1) * POOL_GDIM)
        s = cs[:, POOL_BUF + 1:, sl] - cs[:, POOL_BUF + 1 - win:POOL_BUF + 1 - win + t, sl]
        cnt = jnp.minimum(win, pos + 1).astype(jnp.float32)
        outs.append(s / cnt[None, :, None])
    pooled = jnp.concatenate(outs, axis=-1).astype(xb.dtype) - xb
    pooled = pooled.reshape(b_, t, POOL_GROUPS, POOL_GDIM)
    y = jnp.einsum('btgc,gcd->btgd', pooled, pool_w).reshape(b_, t, G_B) * pool_scale
    return y, ext[:, -POOL_BUF:]


def short_conv_mixer(gate_b, gate_c, h, prefix, w):
    z = gate_c * h
    ext = jnp.concatenate([prefix, z], axis=1)
    y = gate_b * causal_dwconv(ext, w)
    return y, ext[:, -(SCONV_W - 1):]


def conformer_conv_mixer(a, gt, prefix, w, b, ln_g, ln_b):
    z = a * jax.nn.sigmoid(gt)
    ext = jnp.concatenate([prefix, z], axis=1)
    h = causal_dwconv(ext, w) + b
    h = jax.nn.silu(layer_norm(h, ln_g, ln_b))
    return h, ext[:, -(DCONV_W - 1):]


def run_trunk(x, c, start_pos, pool_prefix, sconv_prefix, dconv_prefix, params):
    (ln_in_g, ln_in_b, w_ada, b_ada, ffn1_w_gate, ffn1_w_up, ffn1_w_down, w_in,
     sgu_ln_g, sgu_ln_b, sgu_w, sgu_b, pool_w, pool_scale, sconv_w, dconv_w, dconv_b,
     conv_ln_g, conv_ln_b, out_norm_g, w_out, ffn2_w_gate, ffn2_w_up, ffn2_w_down,
     post_ln_g, post_ln_b) = params
    b_ = x.shape[0]
    x = layer_norm(x, ln_in_g, ln_in_b)
    sgu_states, pool_states, sconv_states, dconv_states = [], [], [], []
    for l in range(DEPTH):
        ada = (jax.nn.silu(c) @ w_ada[l] + b_ada[l]).reshape(b_, N_ADA, D_MODEL)
        sh1, sc1, g1, sh2, sc2, g2, sh3, sc3, g3 = [ada[:, i] for i in range(N_ADA)]
        h = swiglu(modulate(x, sh1, sc1), ffn1_w_gate[l], ffn1_w_up[l], ffn1_w_down[l])
        x = layer_norm(ALPHA * x + 0.5 * (1.0 + g1)[:, None, :] * h, post_ln_g[l, 0], post_ln_b[l, 0])
        proj = modulate(x, sh2, sc2) @ w_in[l]
        u, v, xb, gate_b, gate_c, hc, glu_a, glu_g = jnp.split(proj, SPLIT_POINTS, axis=-1)
        ya, sgu_v = sgu_mixer(u, v, sgu_ln_g[l], sgu_ln_b[l], sgu_w[l], sgu_b[l])
        yb, pool_new = pool_mixer(xb, pool_prefix[l], start_pos, pool_w[l], pool_scale[l])
        yc, sconv_new = short_conv_mixer(gate_b, gate_c, hc, sconv_prefix[l], sconv_w[l])
        yd, dconv_new = conformer_conv_mixer(glu_a, glu_g, dconv_prefix[l], dconv_w[l], dconv_b[l],
                                             conv_ln_g[l], conv_ln_b[l])
        mix = jnp.concatenate([ya, yb, yc, yd], axis=-1)
        mix = rms_norm_groups(mix, out_norm_g[l], N_MIXERS) @ w_out[l]
        x = layer_norm(ALPHA * x + (1.0 + g2)[:, None, :] * mix, post_ln_g[l, 1], post_ln_b[l, 1])
        h = swiglu(modulate(x, sh3, sc3), ffn2_w_gate[l], ffn2_w_up[l], ffn2_w_down[l])
        x = layer_norm(ALPHA * x + 0.5 * (1.0 + g3)[:, None, :] * h, post_ln_g[l, 2], post_ln_b[l, 2])
        sgu_states.append(sgu_v)
        pool_states.append(pool_new)
        sconv_states.append(sconv_new)
        dconv_states.append(dconv_new)
    return (x, jnp.stack(sgu_states, 0), jnp.stack(pool_states, 0),
            jnp.stack(sconv_states, 0), jnp.stack(dconv_states, 0))


def setup_inputs(seed: int = 0) -> dict:
    key = jax.random.key(seed)
    ks = iter(jax.random.split(key, 48))

    def nrm(shape, scale=1.0):
        return jax.random.normal(next(ks), shape, jnp.float32) * scale

    def gain(shape):
        return 1.0 + nrm(shape, 0.05)

    return {
        "x_prompt": nrm((BATCH, SEQ, D_MODEL)),
        "x_sample": nrm((DEC_BATCH, DEC_SEQ, D_MODEL)),
        "state_pool": nrm((DEPTH, DEC_BATCH, POOL_BUF, G_B)),
        "state_sconv": nrm((DEPTH, DEC_BATCH, SCONV_W - 1, G_C)),
        "state_dconv": nrm((DEPTH, DEC_BATCH, DCONV_W - 1, G_D), 0.5),
        "c_prompt": nrm((BATCH, D_MODEL)),
        "c_sample": nrm((DEC_BATCH, D_MODEL)),
        "ln_in_g": gain((D_MODEL,)),
        "ln_in_b": nrm((D_MODEL,), 0.02),
        "w_ada": nrm((DEPTH, D_MODEL, N_ADA * D_MODEL), 0.1 * D_MODEL ** -0.5),
        "b_ada": nrm((DEPTH, N_ADA * D_MODEL), 0.02),
        "ffn1_w_gate": nrm((DEPTH, D_MODEL, D_FF), BETA * D_MODEL ** -0.5),
        "ffn1_w_up": nrm((DEPTH, D_MODEL, D_FF), BETA * D_MODEL ** -0.5),
        "ffn1_w_down": nrm((DEPTH, D_FF, D_MODEL), BETA * D_FF ** -0.5),
        "w_in": nrm((DEPTH, D_MODEL, IN_COLS), D_MODEL ** -0.5),
        "sgu_ln_g": gain((DEPTH, G_A)),
        "sgu_ln_b": nrm((DEPTH, G_A), 0.02),
        "sgu_w": nrm((DEPTH, A_HEADS, CHUNK, CHUNK), 0.5 * CHUNK ** -0.5),
        "sgu_b": gain((DEPTH, A_HEADS, CHUNK)),
        "pool_w": nrm((DEPTH, POOL_GROUPS, POOL_GDIM, POOL_GDIM), POOL_GDIM ** -0.5),
        "pool_scale": gain((DEPTH, G_B)),
        "sconv_w": nrm((DEPTH, SCONV_W, G_C), SCONV_W ** -0.5),
        "dconv_w": nrm((DEPTH, DCONV_W, G_D), DCONV_W ** -0.5),
        "dconv_b": nrm((DEPTH, G_D), 0.02),
        "conv_ln_g": gain((DEPTH, G_D)),
        "conv_ln_b": nrm((DEPTH, G_D), 0.02),
        "out_norm_g": gain((DEPTH, MIX)),
        "w_out": nrm((DEPTH, MIX, D_MODEL), BETA * MIX ** -0.5),
        "ffn2_w_gate": nrm((DEPTH, D_MODEL, D_FF), BETA * D_MODEL ** -0.5),
        "ffn2_w_up": nrm((DEPTH, D_MODEL, D_FF), BETA * D_MODEL ** -0.5),
        "ffn2_w_down": nrm((DEPTH, D_FF, D_MODEL), BETA * D_FF ** -0.5),
        "post_ln_g": gain((DEPTH, 3, D_MODEL)),
        "post_ln_b": nrm((DEPTH, 3, D_MODEL), 0.02),
    }


def reference(x_prompt, x_sample, state_pool, state_sconv, state_dconv, c_prompt, c_sample,
              ln_in_g, ln_in_b, w_ada, b_ada, ffn1_w_gate, ffn1_w_up, ffn1_w_down, w_in,
              sgu_ln_g, sgu_ln_b, sgu_w, sgu_b, pool_w, pool_scale, sconv_w, dconv_w, dconv_b,
              conv_ln_g, conv_ln_b, out_norm_g, w_out, ffn2_w_gate, ffn2_w_up, ffn2_w_down,
              post_ln_g, post_ln_b):
    params = (ln_in_g, ln_in_b, w_ada, b_ada, ffn1_w_gate, ffn1_w_up, ffn1_w_down, w_in,
              sgu_ln_g, sgu_ln_b, sgu_w, sgu_b, pool_w, pool_scale, sconv_w, dconv_w, dconv_b,
              conv_ln_g, conv_ln_b, out_norm_g, w_out, ffn2_w_gate, ffn2_w_up, ffn2_w_down,
              post_ln_g, post_ln_b)
    dt = x_prompt.dtype
    pool_zero = jnp.zeros((DEPTH, BATCH, POOL_BUF, G_B), dt)
    sconv_zero = jnp.zeros((DEPTH, BATCH, SCONV_W - 1, G_C), dt)
    dconv_zero = jnp.zeros((DEPTH, BATCH, DCONV_W - 1, G_D), dt)
    y_prompt, sgu_v_prompt, pool_prompt, sconv_prompt, dconv_prompt = run_trunk(
        x_prompt, c_prompt, 0, pool_zero, sconv_zero, dconv_zero, params)
    y_sample, sgu_v_sample, pool_sample, sconv_sample, dconv_sample = run_trunk(
        x_sample, c_sample, PAST_LEN, state_pool, state_sconv, state_dconv, params)
    return (y_prompt, y_sample, sgu_v_prompt, sgu_v_sample, pool_prompt, pool_sample,
            sconv_prompt, sconv_sample, dconv_prompt, dconv_sample)
```

```python
import functools

import jax
import jax.numpy as jnp
from jax import lax
from jax.experimental import pallas as pl
from jax.experimental.pallas import tpu as pltpu

D_MODEL = 1024
BATCH = 8
SEQ = 2048
DEPTH = 2
DEC_BATCH = 128
DEC_SEQ = 8
PAST_LEN = 16384
G = 256
HEAD_DIM = 64
A_HEADS = G // HEAD_DIM
CHUNK = 128
POOL_WINDOWS = (2, 4, 8, 16)
POOL_GDIM = G // len(POOL_WINDOWS)
POOL_BUF = 15
SCONV_W = 3
DCONV_W = 31
D_FF = 2816
N_ADA = 9
ALPHA = (2 * DEPTH) ** 0.25
IN_COLS = 8 * G
N_SEQ_ALL = DEC_BATCH + BATCH

SUBLANES = 8
POOL_CARRY = 16
SCONV_CARRY = 8
DCONV_CARRY = 32
VMEM_LIMIT_BYTES = 56 * 1024 * 1024

TM_FFN = 512
TM_MIX = 512

BF16 = jnp.bfloat16
F32 = jnp.float32


def _ln(x, g, b, eps=1e-5):
    mu = jnp.mean(x, axis=-1, keepdims=True)
    xc = x - mu
    var = jnp.mean(xc * xc, axis=-1, keepdims=True)
    return xc * lax.rsqrt(var + eps) * g + b


def _gelu(x):
    return 0.5 * x * (1.0 + lax.erf(x * 0.7071067811865476))


def _silu(x):
    return x * jax.nn.sigmoid(x)


def _rows(x, m):
    if m.shape[0] == 1 or m.shape[0] == x.shape[0]:
        return m
    reps = x.shape[0] // m.shape[0]
    return jnp.broadcast_to(m[None], (reps,) + m.shape).reshape(x.shape)


def _dot(a, b):
    return jnp.dot(a, b, preferred_element_type=F32)


def _const_spec(shape, index):
    return pl.BlockSpec(shape, lambda *_: index, pipeline_mode=pl.Buffered(1))


def _ada_kernel(c_ref, w_ref, b_ref, o_ref):
    s = _silu(c_ref[...]).astype(BF16)
    o_ref[...] = _dot(s, w_ref[...].astype(BF16)) + b_ref[...]


def _ada_call(c_all, w_ada, b_ada):
    return pl.pallas_call(
        _ada_kernel,
        grid=(DEPTH, N_ADA),
        in_specs=[
            pl.BlockSpec((N_SEQ_ALL, D_MODEL), lambda l, k: (0, 0)),
            pl.BlockSpec((None, D_MODEL, D_MODEL), lambda l, k: (l, 0, k)),
            pl.BlockSpec((None, None, 1, D_MODEL), lambda l, k: (l, k, 0, 0)),
        ],
        out_specs=pl.BlockSpec((None, None, N_SEQ_ALL, D_MODEL), lambda l, k: (l, k, 0, 0)),
        out_shape=jax.ShapeDtypeStruct((DEPTH, N_ADA, N_SEQ_ALL, D_MODEL), F32),
        compiler_params=pltpu.CompilerParams(
            dimension_semantics=("arbitrary", "arbitrary"), vmem_limit_bytes=VMEM_LIMIT_BYTES),
        name="ada",
    )(c_all, w_ada, b_ada.reshape(DEPTH, N_ADA, 1, D_MODEL))


def _ffn_kernel(x_ref, ada_ref, wg_ref, wu_ref, wd_ref, lng_ref, lnb_ref, ing_ref, inb_ref, o_ref,
                *, layer, sub, first, prompt):
    x = x_ref[...]
    if first:
        x = _ln(x, ing_ref[...], inb_ref[...])
    if prompt:
        b = pl.program_id(0)
        mod = lambda k: ada_ref[k, pl.ds(b, 1), :]
    else:
        mod = lambda k: ada_ref[k]
    k0 = 0 if sub == 0 else 6
    sh, sc, gt = mod(k0), mod(k0 + 1), mod(k0 + 2)
    xm = (x * (1.0 + _rows(x, sc)) + _rows(x, sh)).astype(BF16)
    hg = _dot(xm, wg_ref[...])
    hu = _dot(xm, wu_ref[...])
    h = (_silu(hg) * hu).astype(BF16)
    y = _dot(h, wd_ref[...])
    y = ALPHA * x + (0.5 * (1.0 + _rows(x, gt))) * y
    o_ref[...] = _ln(y, lng_ref[layer, pl.ds(sub, 1), :], lnb_ref[layer, pl.ds(sub, 1), :])


def _ffn_call(x2d, ada, wg, wu, wd, post_g, post_b, in_g, in_b, *, layer, sub, first, prompt):
    rows = x2d.shape[0]
    if prompt:
        nt = SEQ // TM_FFN
        grid = (BATCH, nt)
        x_spec = pl.BlockSpec((TM_FFN, D_MODEL), lambda b, j: (b * nt + j, 0))
        ada_spec = _const_spec((None, N_ADA, BATCH, D_MODEL), (layer, 0, DEC_BATCH // BATCH, 0))
    else:
        grid = (1, 1)
        x_spec = pl.BlockSpec((rows, D_MODEL), lambda b, j: (0, 0))
        ada_spec = _const_spec((None, N_ADA, DEC_BATCH, D_MODEL), (layer, 0, 0, 0))
    kern = functools.partial(_ffn_kernel, layer=layer, sub=sub, first=first, prompt=prompt)
    return pl.pallas_call(
        kern,
        grid=grid,
        in_specs=[
            x_spec, ada_spec,
            _const_spec((None, D_MODEL, D_FF), (layer, 0, 0)),
            _const_spec((None, D_MODEL, D_FF), (layer, 0, 0)),
            _const_spec((None, D_FF, D_MODEL), (layer, 0, 0)),
            _const_spec((DEPTH, 3, D_MODEL), (0, 0, 0)),
            _const_spec((DEPTH, 3, D_MODEL), (0, 0, 0)),
            _const_spec((1, D_MODEL), (0, 0)),
            _const_spec((1, D_MODEL), (0, 0)),
        ],
        out_specs=x_spec,
        out_shape=jax.ShapeDtypeStruct((rows, D_MODEL), F32),
        compiler_params=pltpu.CompilerParams(
            dimension_semantics=("arbitrary", "arbitrary"), vmem_limit_bytes=VMEM_LIMIT_BYTES),
        name=f"ffn{sub}_{'p' if prompt else 's'}{layer}",
    )(x2d, ada, wg, wu, wd, post_g, post_b, in_g, in_b)


def _pool_lane_consts():
    lane = lax.broadcasted_iota(jnp.int32, (1, G), 1)
    grp = lane // POOL_GDIM
    win = jnp.where(grp == 0, float(POOL_WINDOWS[0]),
                    jnp.where(grp == 1, float(POOL_WINDOWS[1]),
                              jnp.where(grp == 2, float(POOL_WINDOWS[2]), float(POOL_WINDOWS[3]))))
    return grp, win


def _pool_select(grp, s2, s4, s8, s16):
    return jnp.where(grp == 0, s2, jnp.where(grp == 1, s4, jnp.where(grp == 2, s8, s16)))


def _rms_group(y, g, eps=1e-6):
    return y * lax.rsqrt(jnp.mean(y * y, axis=-1, keepdims=True) + eps) * g


def _mixer_tail(x, ya, yb, yc, yd, gt, wout_ref, onorm_ref, lng_ref, lnb_ref, layer):
    parts = [ya, yb, yc, yd]
    normed = [_rms_group(p, onorm_ref[pl.ds(layer, 1), i * G:(i + 1) * G]).astype(BF16)
              for i, p in enumerate(parts)]
    mix = _dot(jnp.concatenate(normed, axis=-1), wout_ref[...])
    y = ALPHA * x + (1.0 + _rows(x, gt)) * mix
    return _ln(y, lng_ref[layer, pl.ds(1, 1), :], lnb_ref[layer, pl.ds(1, 1), :])


def _mixer_prompt_kernel(x_ref, ada_ref, win_ref, wout_ref, sguw_ref, sgub_ref, sgug_ref, sgulb_ref,
                         poolw_ref, pools_ref, sconvw_ref, dconvw_ref, dconvb_ref, cvg_ref, cvb_ref,
                         onorm_ref, lng_ref, lnb_ref,
                         o_ref, sguv_ref, poolst_ref, sconvst_ref, dconvst_ref,
                         pool_ext, sconv_ext, dconv_ext, *, layer):
    tm = x_ref.shape[0]
    b = pl.program_id(0)
    j = pl.program_id(1)
    lrow = pl.ds(layer, 1)

    @pl.when(j == 0)
    def _():
        pool_ext[0:POOL_CARRY, :] = jnp.zeros((POOL_CARRY, G), F32)
        sconv_ext[0:SCONV_CARRY, :] = jnp.zeros((SCONV_CARRY, G), F32)
        dconv_ext[0:DCONV_CARRY, :] = jnp.zeros((DCONV_CARRY, G), F32)

    @pl.when(j > 0)
    def _():
        pool_ext[0:POOL_CARRY, :] = pool_ext[tm:tm + POOL_CARRY, :]
        sconv_ext[0:SCONV_CARRY, :] = sconv_ext[tm:tm + SCONV_CARRY, :]
        dconv_ext[0:DCONV_CARRY, :] = dconv_ext[tm:tm + DCONV_CARRY, :]

    x = x_ref[...]
    mod = lambda k: ada_ref[k, pl.ds(b, 1), :]
    sh, sc, gt = mod(3), mod(4), mod(5)
    xm = (x * (1.0 + sc) + sh).astype(BF16)
    proj = _dot(xm, win_ref[...])
    col = lambda i: proj[:, i * G:(i + 1) * G]

    u = _gelu(col(0))
    v = _ln(_gelu(col(1)), sgug_ref[lrow, :], sgulb_ref[lrow, :])
    sguv_ref[...] = v[tm - CHUNK:, :]
    r_i = lax.broadcasted_iota(jnp.int32, (CHUNK, CHUNK), 0)
    c_i = lax.broadcasted_iota(jnp.int32, (CHUNK, CHUNK), 1)
    wcat = jnp.concatenate([jnp.where(r_i >= c_i, sguw_ref[h], 0.0) for h in range(A_HEADS)],
                           axis=1).astype(BF16)
    head = lax.broadcasted_iota(jnp.int32, (1, G), 1) // HEAD_DIM
    bias = sgub_ref[...]
    mixed = []
    for c in range(tm // CHUNK):
        vc = v[c * CHUNK:(c + 1) * CHUNK, :]
        rhs = jnp.concatenate([jnp.where(head == h, vc, 0.0) for h in range(A_HEADS)],
                              axis=0).astype(BF16)
        mixed.append(_dot(wcat, rhs) + bias)
    ya = u * jnp.concatenate(mixed, axis=0)

    xb = col(2)
    pool_ext[POOL_CARRY:, :] = xb
    e = pool_ext[...]
    s2 = e + pltpu.roll(e, 1, 0)
    s4 = s2 + pltpu.roll(s2, 2, 0)
    s8 = s4 + pltpu.roll(s4, 4, 0)
    s16 = s8 + pltpu.roll(s8, 8, 0)
    grp, win = _pool_lane_consts()
    ssel = _pool_select(grp, s2, s4, s8, s16)[POOL_CARRY:, :]
    pos = j * tm + lax.broadcasted_iota(jnp.int32, (tm, 1), 0)
    cnt = jnp.minimum(win, (pos + 1).astype(F32))
    pooled = (ssel / cnt - xb).astype(BF16)
    yb = _dot(pooled, poolw_ref[...]) * pools_ref[lrow, :]
    poolst_ref[...] = pool_ext[tm + POOL_CARRY - POOL_BUF:tm + POOL_CARRY, :]

    zc = col(4) * col(5)
    sconv_ext[SCONV_CARRY:, :] = zc
    es = sconv_ext[...]
    yconv = (sconvw_ref[layer, pl.ds(2, 1), :] * es
             + sconvw_ref[layer, pl.ds(1, 1), :] * pltpu.roll(es, 1, 0)
             + sconvw_ref[layer, pl.ds(0, 1), :] * pltpu.roll(es, 2, 0))
    yc = col(3) * yconv[SCONV_CARRY:, :]
    sconvst_ref[...] = sconv_ext[tm + SCONV_CARRY - (SCONV_W - 1):tm + SCONV_CARRY, :]

    zd = col(6) * jax.nn.sigmoid(col(7))
    dconv_ext[DCONV_CARRY:, :] = zd
    ed = dconv_ext[...]
    acc = jnp.zeros((tm, G), F32)
    for r in range(SUBLANES):
        er = ed if r == 0 else pltpu.roll(ed, r, 0)
        for q in range(DCONV_CARRY // SUBLANES):
            tap = SUBLANES * q + r
            if tap >= DCONV_W:
                continue
            lo = DCONV_CARRY - SUBLANES * q
            acc = acc + dconvw_ref[layer, pl.ds(DCONV_W - 1 - tap, 1), :] * er[lo:lo + tm, :]
    hd = _ln(acc + dconvb_ref[lrow, :], cvg_ref[lrow, :], cvb_ref[lrow, :])
    yd = _silu(hd)
    dconvst_ref[...] = dconv_ext[tm + DCONV_CARRY - (DCONV_W - 1):tm + DCONV_CARRY, :]

    o_ref[...] = _mixer_tail(x, ya, yb, yc, yd, gt, wout_ref, onorm_ref, lng_ref, lnb_ref, layer)


def _mixer_prompt_call(x2d, ada, w_in, w_out, sgu_w, sgu_bias, small, *, layer):
    (sgu_ln_g, sgu_ln_b, pool_bd, pool_scale, sconv_w, dconv_w, dconv_b, conv_ln_g, conv_ln_b,
     out_norm_g, post_g, post_b) = small
    tm = TM_MIX
    nt = SEQ // tm
    x_spec = pl.BlockSpec((tm, D_MODEL), lambda b, j: (b * nt + j, 0))
    st_spec = lambda n: pl.BlockSpec((None, n, G), lambda b, j: (b, 0, 0))
    full2 = lambda a: _const_spec(a.shape, (0,) * a.ndim)
    return pl.pallas_call(
        functools.partial(_mixer_prompt_kernel, layer=layer),
        grid=(BATCH, nt),
        in_specs=[
            x_spec,
            _const_spec((None, N_ADA, BATCH, D_MODEL), (layer, 0, DEC_BATCH // BATCH, 0)),
            _const_spec((None, D_MODEL, IN_COLS), (layer, 0, 0)),
            _const_spec((None, D_MODEL, D_MODEL), (layer, 0, 0)),
            _const_spec((None, A_HEADS, CHUNK, CHUNK), (layer, 0, 0, 0)),
            _const_spec((None, CHUNK, G), (layer, 0, 0)),
            full2(sgu_ln_g), full2(sgu_ln_b),
            _const_spec((None, G, G), (layer, 0, 0)),
            full2(pool_scale), full2(sconv_w), full2(dconv_w), full2(dconv_b),
            full2(conv_ln_g), full2(conv_ln_b), full2(out_norm_g), full2(post_g), full2(post_b),
        ],
        out_specs=[x_spec, st_spec(CHUNK), st_spec(POOL_BUF), st_spec(SCONV_W - 1), st_spec(DCONV_W - 1)],
        out_shape=[
            jax.ShapeDtypeStruct((BATCH * SEQ, D_MODEL), F32),
            jax.ShapeDtypeStruct((BATCH, CHUNK, G), F32),
            jax.ShapeDtypeStruct((BATCH, POOL_BUF, G), F32),
            jax.ShapeDtypeStruct((BATCH, SCONV_W - 1, G), F32),
            jax.ShapeDtypeStruct((BATCH, DCONV_W - 1, G), F32),
        ],
        scratch_shapes=[
            pltpu.VMEM((tm + POOL_CARRY, G), F32),
            pltpu.VMEM((tm + SCONV_CARRY, G), F32),
            pltpu.VMEM((tm + DCONV_CARRY, G), F32),
        ],
        compiler_params=pltpu.CompilerParams(
            dimension_semantics=("arbitrary", "arbitrary"), vmem_limit_bytes=VMEM_LIMIT_BYTES),
        name=f"mixer_p{layer}",
    )(x2d, ada, w_in, w_out, sgu_w, sgu_bias, sgu_ln_g, sgu_ln_b, pool_bd, pool_scale, sconv_w,
      dconv_w, dconv_b, conv_ln_g, conv_ln_b, out_norm_g, post_g, post_b)


def _mixer_sample_kernel(x_ref, ada_ref, win_ref, wout_ref, sguw_ref, sgub_ref, sgug_ref, sgulb_ref,
                         poolw_ref, pools_ref, sconvw_ref, dconvw_ref, dconvb_ref, cvg_ref, cvb_ref,
                         onorm_ref, lng_ref, lnb_ref, poolin_ref, sconvin_ref, dconvin_ref,
                         o_ref, sguv_ref, poolst_ref, sconvst_ref, dconvst_ref, *, layer):
    nb = DEC_BATCH
    lrow = pl.ds(layer, 1)
    x = x_ref[...]
    sh, sc, gt = ada_ref[3], ada_ref[4], ada_ref[5]
    xm = (x * (1.0 + _rows(x, sc)) + _rows(x, sh)).astype(BF16)
    proj = _dot(xm, win_ref[...])
    col = lambda i: proj[:, i * G:(i + 1) * G]
    slab = lambda a, t: a[t * nb:(t + 1) * nb, :]

    u = _gelu(col(0))
    v = _ln(_gelu(col(1)), sgug_ref[lrow, :], sgulb_ref[lrow, :])
    for t in range(DEC_SEQ):
        sguv_ref[t] = slab(v, t)
    mixed = []
    for t in range(DEC_SEQ):
        m = jnp.zeros((nb, G), F32) + sgub_ref[pl.ds(t, 1), :]
        for s in range(t + 1):
            m = m + sguw_ref[t, pl.ds(s, 1), :] * slab(v, s)
        mixed.append(m)
    ya = u * jnp.concatenate(mixed, axis=0)

    xb = col(2)
    e = [poolin_ref[i] for i in range(POOL_BUF)] + [slab(xb, t) for t in range(DEC_SEQ)]
    n_e = len(e)
    s2 = [e[i] + e[i - 1] if i >= 1 else None for i in range(n_e)]
    s4 = [s2[i] + s2[i - 2] if i >= 3 else None for i in range(n_e)]
    s8 = [s4[i] + s4[i - 4] if i >= 7 else None for i in range(n_e)]
    s16 = [s8[i] + s8[i - 8] if i >= 15 else None for i in range(n_e)]
    grp, win = _pool_lane_consts()
    cnt = jnp.minimum(win, float(PAST_LEN + 1))
    sel = jnp.concatenate([_pool_select(grp, s2[i], s4[i], s8[i], s16[i])
                           for i in range(POOL_BUF, n_e)], axis=0)
    pooled = (sel / cnt - xb).astype(BF16)
    yb = _dot(pooled, poolw_ref[...]) * pools_ref[lrow, :]
    for i in range(POOL_BUF):
        poolst_ref[i] = e[n_e - POOL_BUF + i]

    zc = col(4) * col(5)
    es = [sconvin_ref[i] for i in range(SCONV_W - 1)] + [slab(zc, t) for t in range(DEC_SEQ)]
    yconv = jnp.concatenate(
        [sum(sconvw_ref[layer, pl.ds(k, 1), :] * es[t + k] for k in range(SCONV_W))
         for t in range(DEC_SEQ)], axis=0)
    yc = col(3) * yconv
    for i in range(SCONV_W - 1):
        sconvst_ref[i] = es[len(es) - (SCONV_W - 1) + i]

    zd = col(6) * jax.nn.sigmoid(col(7))
    for i in range(DCONV_W - 1 - DEC_SEQ):
        dconvst_ref[i] = dconvin_ref[i + DEC_SEQ]
    for t in range(DEC_SEQ):
        dconvst_ref[DCONV_W - 1 - DEC_SEQ + t] = slab(zd, t)
    conv = []
    for t in range(DEC_SEQ):
        a = jnp.zeros((nb, G), F32)
        for k in range(DCONV_W):
            i = t + k
            src = dconvin_ref[i] if i < DCONV_W - 1 else slab(zd, i - (DCONV_W - 1))
            a = a + dconvw_ref[layer, pl.ds(k, 1), :] * src
        conv.append(a)
    hd = _ln(jnp.concatenate(conv, axis=0) + dconvb_ref[lrow, :], cvg_ref[lrow, :], cvb_ref[lrow, :])
    yd = _silu(hd)

    o_ref[...] = _mixer_tail(x, ya, yb, yc, yd, gt, wout_ref, onorm_ref, lng_ref, lnb_ref, layer)


def _mixer_sample_call(x2d, ada, w_in, w_out, sgu_wsm, sgu_bias, small, pool_in, sconv_in, dconv_in,
                       *, layer):
    (sgu_ln_g, sgu_ln_b, pool_bd, pool_scale, sconv_w, dconv_w, dconv_b, conv_ln_g, conv_ln_b,
     out_norm_g, post_g, post_b) = small
    rows = DEC_SEQ * DEC_BATCH
    full2 = lambda a: _const_spec(a.shape, (0,) * a.ndim)
    st = lambda n: _const_spec((None, n, DEC_BATCH, G), (layer, 0, 0, 0))
    st_out = lambda n: pl.BlockSpec((n, DEC_BATCH, G), lambda i: (0, 0, 0))
    return pl.pallas_call(
        functools.partial(_mixer_sample_kernel, layer=layer),
        grid=(1,),
        in_specs=[
            _const_spec((rows, D_MODEL), (0, 0)),
            _const_spec((None, N_ADA, DEC_BATCH, D_MODEL), (layer, 0, 0, 0)),
            _const_spec((None, D_MODEL, IN_COLS), (layer, 0, 0)),
            _const_spec((None, D_MODEL, D_MODEL), (layer, 0, 0)),
            _const_spec((None, DEC_SEQ, DEC_SEQ, G), (layer, 0, 0, 0)),
            _const_spec((None, DEC_SEQ, G), (layer, 0, 0)),
            full2(sgu_ln_g), full2(sgu_ln_b),
            _const_spec((None, G, G), (layer, 0, 0)),
            full2(pool_scale), full2(sconv_w), full2(dconv_w), full2(dconv_b),
            full2(conv_ln_g), full2(conv_ln_b), full2(out_norm_g), full2(post_g), full2(post_b),
            st(POOL_BUF), st(SCONV_W - 1), st(DCONV_W - 1),
        ],
        out_specs=[pl.BlockSpec((rows, D_MODEL), lambda i: (0, 0)),
                   st_out(DEC_SEQ), st_out(POOL_BUF), st_out(SCONV_W - 1), st_out(DCONV_W - 1)],
        out_shape=[
            jax.ShapeDtypeStruct((rows, D_MODEL), F32),
            jax.ShapeDtypeStruct((DEC_SEQ, DEC_BATCH, G), F32),
            jax.ShapeDtypeStruct((POOL_BUF, DEC_BATCH, G), F32),
            jax.ShapeDtypeStruct((SCONV_W - 1, DEC_BATCH, G), F32),
            jax.ShapeDtypeStruct((DCONV_W - 1, DEC_BATCH, G), F32),
        ],
        compiler_params=pltpu.CompilerParams(
            dimension_semantics=("arbitrary",), vmem_limit_bytes=VMEM_LIMIT_BYTES),
        name=f"mixer_s{layer}",
    )(x2d, ada, w_in, w_out, sgu_wsm, sgu_bias, sgu_ln_g, sgu_ln_b, pool_bd, pool_scale, sconv_w,
      dconv_w, dconv_b, conv_ln_g, conv_ln_b, out_norm_g, post_g, post_b, pool_in, sconv_in, dconv_in)


def kernel(x_prompt, x_sample, state_pool, state_sconv, state_dconv, c_prompt, c_sample, ln_in_g, ln_in_b, w_ada, b_ada, ffn1_w_gate, ffn1_w_up, ffn1_w_down, w_in, sgu_ln_g, sgu_ln_b, sgu_w, sgu_b, pool_w, pool_scale, sconv_w, dconv_w, dconv_b, conv_ln_g, conv_ln_b, out_norm_g, w_out, ffn2_w_gate, ffn2_w_up, ffn2_w_down, post_ln_g, post_ln_b):
    wg1, wu1, wd1 = ffn1_w_gate.astype(BF16), ffn1_w_up.astype(BF16), ffn1_w_down.astype(BF16)
    wg2, wu2, wd2 = ffn2_w_gate.astype(BF16), ffn2_w_up.astype(BF16), ffn2_w_down.astype(BF16)
    w_in_b, w_out_b = w_in.astype(BF16), w_out.astype(BF16)
    eye = jnp.eye(len(POOL_WINDOWS), dtype=F32)
    pool_bd = jnp.einsum('lgcd,gh->lgchd', pool_w, eye).reshape(DEPTH, G, G).astype(BF16)
    sgu_bias = jnp.repeat(jnp.transpose(sgu_b, (0, 2, 1)), HEAD_DIM, axis=-1)
    sgu_wsm = jnp.repeat(jnp.transpose(sgu_w[:, :, :DEC_SEQ, :DEC_SEQ], (0, 2, 3, 1)),
                         HEAD_DIM, axis=-1)
    in_g, in_b = ln_in_g.reshape(1, D_MODEL), ln_in_b.reshape(1, D_MODEL)
    small = (sgu_ln_g, sgu_ln_b, pool_bd, pool_scale, sconv_w, dconv_w, dconv_b, conv_ln_g,
             conv_ln_b, out_norm_g, post_ln_g, post_ln_b)

    ada = _ada_call(jnp.concatenate([c_sample, c_prompt], axis=0), w_ada, b_ada)

    xs = jnp.transpose(x_sample, (1, 0, 2)).reshape(DEC_SEQ * DEC_BATCH, D_MODEL)
    pool_in = jnp.transpose(state_pool, (0, 2, 1, 3))
    sconv_in = jnp.transpose(state_sconv, (0, 2, 1, 3))
    dconv_in = jnp.transpose(state_dconv, (0, 2, 1, 3))
    xp = x_prompt.reshape(BATCH * SEQ, D_MODEL)

    outs_p, outs_s = [], []
    for l in range(DEPTH):
        ffn = functools.partial(_ffn_call, post_g=post_ln_g, post_b=post_ln_b, in_g=in_g, in_b=in_b,
                                layer=l)
        xp = ffn(xp, ada, wg1, wu1, wd1, sub=0, first=(l == 0), prompt=True)
        xs = ffn(xs, ada, wg1, wu1, wd1, sub=0, first=(l == 0), prompt=False)
        xp, *st_p = _mixer_prompt_call(xp, ada, w_in_b, w_out_b, sgu_w, sgu_bias, small, layer=l)
        xs, *st_s = _mixer_sample_call(xs, ada, w_in_b, w_out_b, sgu_wsm, sgu_bias, small,
                                       pool_in, sconv_in, dconv_in, layer=l)
        xp = ffn(xp, ada, wg2, wu2, wd2, sub=2, first=False, prompt=True)
        xs = ffn(xs, ada, wg2, wu2, wd2, sub=2, first=False, prompt=False)
        outs_p.append(st_p)
        outs_s.append(st_s)

    y_prompt = xp.reshape(BATCH, SEQ, D_MODEL)
    y_sample = jnp.transpose(xs.reshape(DEC_SEQ, DEC_BATCH, D_MODEL), (1, 0, 2))
    stack_p = lambda i: jnp.stack([o[i] for o in outs_p], axis=0)
    stack_s = lambda i: jnp.transpose(jnp.stack([o[i] for o in outs_s], axis=0), (0, 2, 1, 3))
    return (y_prompt, y_sample, stack_p(0), stack_s(0), stack_p(1), stack_s(1),
            stack_p(2), stack_s(2), stack_p(3), stack_s(3))
```

```python
import functools

import jax
import jax.numpy as jnp
from jax import lax
from jax.experimental import pallas as pl
from jax.experimental.pallas import tpu as pltpu

D_MODEL = 1024
BATCH = 8
SEQ = 2048
DEPTH = 2
DEC_BATCH = 128
DEC_SEQ = 8
PAST_LEN = 16384
G = 256
HEAD_DIM = 64
A_HEADS = G // HEAD_DIM
CHUNK = 128
POOL_WINDOWS = (2, 4, 8, 16)
POOL_GDIM = G // len(POOL_WINDOWS)
POOL_BUF = 15
SCONV_W = 3
DCONV_W = 31
D_FF = 2816
N_ADA = 9
ALPHA = (2 * DEPTH) ** 0.25
IN_COLS = 8 * G
N_SEQ_ALL = DEC_BATCH + BATCH
ADA_PER_STEP = 3

SUBLANES = 8
BF16_SUBLANES = 16
POOL_CARRY = 16
SCONV_CARRY = 8
DCONV_CARRY = 32
VMEM_LIMIT_BYTES = 56 * 1024 * 1024

TM_FFN = 512
TM_MIX = 512
SUB_MIX = 256

BF16 = jnp.bfloat16
F32 = jnp.float32


def _ln(x, g, b, eps=1e-5):
    mu = jnp.mean(x, axis=-1, keepdims=True)
    xc = x - mu
    var = jnp.mean(xc * xc, axis=-1, keepdims=True)
    return xc * lax.rsqrt(var + eps) * g + b


def _gelu(x):
    return 0.5 * x * (1.0 + lax.erf(x * 0.7071067811865476))


def _silu(x):
    return x * jax.nn.sigmoid(x)


def _rows(x, m):
    if m.shape[0] == 1 or m.shape[0] == x.shape[0]:
        return m
    reps = x.shape[0] // m.shape[0]
    return jnp.broadcast_to(m[None], (reps,) + m.shape).reshape(x.shape)


def _dot(a, b):
    return jnp.dot(a, b, preferred_element_type=F32)


def _const_spec(shape, index):
    return pl.BlockSpec(shape, lambda *_: index, pipeline_mode=pl.Buffered(1))


def _ada_kernel(c_ref, w_ref, b_ref, o_ref):
    s = _silu(c_ref[...]).astype(BF16)
    y = _dot(s, w_ref[...].astype(BF16))
    for k in range(ADA_PER_STEP):
        o_ref[k] = y[:, k * D_MODEL:(k + 1) * D_MODEL] + b_ref[k]


def _ada_call(c_all, w_ada, b_ada):
    return pl.pallas_call(
        _ada_kernel,
        grid=(DEPTH, N_ADA // ADA_PER_STEP),
        in_specs=[
            pl.BlockSpec((N_SEQ_ALL, D_MODEL), lambda l, k: (0, 0)),
            pl.BlockSpec((None, D_MODEL, ADA_PER_STEP * D_MODEL), lambda l, k: (l, 0, k)),
            pl.BlockSpec((None, ADA_PER_STEP, 1, D_MODEL), lambda l, k: (l, k, 0, 0)),
        ],
        out_specs=pl.BlockSpec((None, ADA_PER_STEP, N_SEQ_ALL, D_MODEL), lambda l, k: (l, k, 0, 0)),
        out_shape=jax.ShapeDtypeStruct((DEPTH, N_ADA, N_SEQ_ALL, D_MODEL), F32),
        compiler_params=pltpu.CompilerParams(
            dimension_semantics=("arbitrary", "arbitrary"), vmem_limit_bytes=VMEM_LIMIT_BYTES),
        name="ada",
    )(c_all, w_ada, b_ada.reshape(DEPTH, N_ADA, 1, D_MODEL))


def _ffn_math(x, mod, wg_ref, wu_ref, wd_ref, lng_ref, lnb_ref, *, layer, sub):
    k0 = 0 if sub == 0 else 6
    sh, sc, gt = mod(k0), mod(k0 + 1), mod(k0 + 2)
    xm = (x * (1.0 + _rows(x, sc)) + _rows(x, sh)).astype(BF16)
    hg = _dot(xm, wg_ref[...])
    hu = _dot(xm, wu_ref[...])
    h = (_silu(hg) * hu).astype(BF16)
    gain = 0.5 * (1.0 + _rows(x, gt))
    half = x.shape[0] // 2
    outs = []
    for r in (slice(0, half), slice(half, 2 * half)):
        y = ALPHA * x[r] + (gain if gain.shape[0] == 1 else gain[r]) * _dot(h[r], wd_ref[...])
        outs.append(_ln(y, lng_ref[layer, pl.ds(sub, 1), :], lnb_ref[layer, pl.ds(sub, 1), :]))
    return jnp.concatenate(outs, axis=0)


def _ffn_kernel(x_ref, ada_ref, wg_ref, wu_ref, wd_ref, lng_ref, lnb_ref, ing_ref, inb_ref, *rest,
                layer, sub, first, prompt, n_side):
    side_in, o_ref, side_out = rest[:n_side], rest[n_side], rest[n_side + 1:]
    x = x_ref[...]
    if first:
        x = _ln(x, ing_ref[...], inb_ref[...])
    if prompt:
        b = pl.program_id(0)
        mod = lambda k: ada_ref[k, pl.ds(b, 1), :]
    else:
        mod = lambda k: ada_ref[k]
    o_ref[...] = _ffn_math(x, mod, wg_ref, wu_ref, wd_ref, lng_ref, lnb_ref, layer=layer, sub=sub)
    for src, dst in zip(side_in, side_out):
        dst[...] = src[...].astype(BF16)


def _ffn_call(x2d, ada, wg, wu, wd, post_g, post_b, in_g, in_b, *, layer, sub, first, prompt, side=()):
    rows = x2d.shape[0]
    if prompt:
        nt = SEQ // TM_FFN
        grid = (BATCH, nt)
        x_spec = pl.BlockSpec((TM_FFN, D_MODEL), lambda b, j: (b * nt + j, 0))
        ada_spec = _const_spec((None, N_ADA, BATCH, D_MODEL), (layer, 0, DEC_BATCH // BATCH, 0))
    else:
        assert not side
        grid = (1, 1)
        x_spec = pl.BlockSpec((rows, D_MODEL), lambda b, j: (0, 0))
        ada_spec = _const_spec((None, N_ADA, DEC_BATCH, D_MODEL), (layer, 0, 0, 0))
    side_in_specs, side_out_specs, side_shapes = [], [], []
    for w, wl in side:
        k_dim, n_dim = w.shape[1:]
        steps = grid[0] * grid[1]
        while (k_dim // steps) % BF16_SUBLANES:
            steps //= 2
        per = grid[0] * grid[1] // steps
        chunk = lambda b, j, per=per: (b * grid[1] + j) // per
        side_in_specs.append(pl.BlockSpec((None, k_dim // steps, n_dim),
                                          lambda b, j, wl=wl, chunk=chunk: (wl, chunk(b, j), 0)))
        side_out_specs.append(pl.BlockSpec((k_dim // steps, n_dim),
                                           lambda b, j, chunk=chunk: (chunk(b, j), 0)))
        side_shapes.append(jax.ShapeDtypeStruct((k_dim, n_dim), BF16))
    kern = functools.partial(_ffn_kernel, layer=layer, sub=sub, first=first, prompt=prompt,
                             n_side=len(side))
    res = pl.pallas_call(
        kern,
        grid=grid,
        in_specs=[
            x_spec, ada_spec,
            _const_spec((D_MODEL, D_FF), (0, 0)),
            _const_spec((D_MODEL, D_FF), (0, 0)),
            _const_spec((D_FF, D_MODEL), (0, 0)),
            _const_spec((DEPTH, 3, D_MODEL), (0, 0, 0)),
            _const_spec((DEPTH, 3, D_MODEL), (0, 0, 0)),
            _const_spec((1, D_MODEL), (0, 0)),
            _const_spec((1, D_MODEL), (0, 0)),
        ] + side_in_specs,
        out_specs=[x_spec] + side_out_specs,
        out_shape=[jax.ShapeDtypeStruct((rows, D_MODEL), F32)] + side_shapes,
        compiler_params=pltpu.CompilerParams(
            dimension_semantics=("arbitrary", "arbitrary"), vmem_limit_bytes=VMEM_LIMIT_BYTES),
        name=f"ffn{sub}_{'p' if prompt else 's'}{layer}",
    )(x2d, ada, wg, wu, wd, post_g, post_b, in_g, in_b, *[w for w, _ in side])
    return res[0], list(res[1:])


def _pool_lane_consts():
    lane = lax.broadcasted_iota(jnp.int32, (1, G), 1)
    grp = lane // POOL_GDIM
    win = jnp.where(grp == 0, float(POOL_WINDOWS[0]),
                    jnp.where(grp == 1, float(POOL_WINDOWS[1]),
                              jnp.where(grp == 2, float(POOL_WINDOWS[2]), float(POOL_WINDOWS[3]))))
    return grp, win


def _pool_select(grp, s2, s4, s8, s16):
    return jnp.where(grp == 0, s2, jnp.where(grp == 1, s4, jnp.where(grp == 2, s8, s16)))


def _rms_group(y, g, eps=1e-6):
    return y * lax.rsqrt(jnp.mean(y * y, axis=-1, keepdims=True) + eps) * g


def _mixer_front(x, mod, win_ref):
    sh, sc = mod(3), mod(4)
    xm = (x * (1.0 + _rows(x, sc)) + _rows(x, sh)).astype(BF16)
    return _dot(xm, win_ref[...])


def _mixer_norm(ya, yb, yc, yd, onorm_ref, layer):
    parts = [ya, yb, yc, yd]
    normed = [_rms_group(p, onorm_ref[pl.ds(layer, 1), i * G:(i + 1) * G]).astype(BF16)
              for i, p in enumerate(parts)]
    return jnp.concatenate(normed, axis=-1)


def _mixer_back(x, mixn, mod, wout_ref, lng_ref, lnb_ref, layer):
    y = ALPHA * x + (1.0 + _rows(x, mod(5))) * _dot(mixn, wout_ref[...])
    return _ln(y, lng_ref[layer, pl.ds(1, 1), :], lnb_ref[layer, pl.ds(1, 1), :])


def _mixer_prompt_mid(proj, pos0, r0, last, sguw_ref, sgub_ref, sgug_ref, sgulb_ref,
                      poolw_ref, pools_ref, sconvw_ref, dconvw_ref, dconvb_ref, cvg_ref, cvb_ref,
                      onorm_ref, sguv_ref, poolst_ref, sconvst_ref, dconvst_ref,
                      pool_ext, sconv_ext, dconv_ext, *, layer):
    n = proj.shape[0]
    lrow = pl.ds(layer, 1)
    col = lambda i: proj[:, i * G:(i + 1) * G]

    u = _gelu(col(0))
    v = _ln(_gelu(col(1)), sgug_ref[lrow, :], sgulb_ref[lrow, :])
    if last:
        sguv_ref[...] = v[n - CHUNK:, :]
    r_i = lax.broadcasted_iota(jnp.int32, (CHUNK, CHUNK), 0)
    c_i = lax.broadcasted_iota(jnp.int32, (CHUNK, CHUNK), 1)
    wcat = jnp.concatenate([jnp.where(r_i >= c_i, sguw_ref[h], 0.0) for h in range(A_HEADS)],
                           axis=1).astype(BF16)
    head = lax.broadcasted_iota(jnp.int32, (1, G), 1) // HEAD_DIM
    bias = sgub_ref[...]
    mixed = []
    for c in range(n // CHUNK):
        vc = v[c * CHUNK:(c + 1) * CHUNK, :]
        rhs = jnp.concatenate([jnp.where(head == h, vc, 0.0) for h in range(A_HEADS)],
                              axis=0).astype(BF16)
        mixed.append(_dot(wcat, rhs) + bias)
    ya = u * jnp.concatenate(mixed, axis=0)

    xb = col(2)
    pool_ext[r0 + POOL_CARRY:r0 + POOL_CARRY + n, :] = xb
    e = pool_ext[r0:r0 + POOL_CARRY + n, :]
    s2 = e + pltpu.roll(e, 1, 0)
    s4 = s2 + pltpu.roll(s2, 2, 0)
    s8 = s4 + pltpu.roll(s4, 4, 0)
    s16 = s8 + pltpu.roll(s8, 8, 0)
    grp, win = _pool_lane_consts()
    ssel = _pool_select(grp, s2, s4, s8, s16)[POOL_CARRY:, :]
    pos = pos0 + lax.broadcasted_iota(jnp.int32, (n, 1), 0)
    cnt = jnp.minimum(win, (pos + 1).astype(F32))
    pooled = (ssel / cnt - xb).astype(BF16)
    yb = _dot(pooled, poolw_ref[...]) * pools_ref[lrow, :]
    if last:
        poolst_ref[...] = pool_ext[r0 + n + POOL_CARRY - POOL_BUF:r0 + n + POOL_CARRY, :]

    zc = col(4) * col(5)
    sconv_ext[r0 + SCONV_CARRY:r0 + SCONV_CARRY + n, :] = zc
    es = sconv_ext[r0:r0 + SCONV_CARRY + n, :]
    yconv = (sconvw_ref[layer, pl.ds(2, 1), :] * es
             + sconvw_ref[layer, pl.ds(1, 1), :] * pltpu.roll(es, 1, 0)
             + sconvw_ref[layer, pl.ds(0, 1), :] * pltpu.roll(es, 2, 0))
    yc = col(3) * yconv[SCONV_CARRY:, :]
    if last:
        sconvst_ref[...] = sconv_ext[r0 + n + SCONV_CARRY - (SCONV_W - 1):r0 + n + SCONV_CARRY, :]

    zd = col(6) * jax.nn.sigmoid(col(7))
    dconv_ext[r0 + DCONV_CARRY:r0 + DCONV_CARRY + n, :] = zd
    if last:
        dconvst_ref[...] = dconv_ext[r0 + n + DCONV_CARRY - (DCONV_W - 1):r0 + n + DCONV_CARRY, :]
    ed = dconv_ext[r0:r0 + DCONV_CARRY + n, :]
    acc = jnp.zeros((n, G), F32)
    for r in range(SUBLANES):
        er = ed if r == 0 else pltpu.roll(ed, r, 0)
        for q in range(DCONV_CARRY // SUBLANES):
            tap = SUBLANES * q + r
            if tap >= DCONV_W:
                continue
            lo = DCONV_CARRY - SUBLANES * q
            acc = acc + dconvw_ref[layer, pl.ds(DCONV_W - 1 - tap, 1), :] * er[lo:lo + n, :]
    hd = _ln(acc + dconvb_ref[lrow, :], cvg_ref[lrow, :], cvb_ref[lrow, :])
    yd = _silu(hd)

    return _mixer_norm(ya, yb, yc, yd, onorm_ref, layer)


def _mixer_prompt_kernel(x_ref, ada_ref, win_ref, wout_ref, sguw_ref, sgub_ref, sgug_ref, sgulb_ref,
                         poolw_ref, pools_ref, sconvw_ref, dconvw_ref, dconvb_ref, cvg_ref, cvb_ref,
                         onorm_ref, lng_ref, lnb_ref,
                         o_ref, sguv_ref, poolst_ref, sconvst_ref, dconvst_ref,
                         pool_ext, sconv_ext, dconv_ext, *, layer, sub_rows):
    tm = x_ref.shape[0]
    b = pl.program_id(0)
    j = pl.program_id(1)

    @pl.when(j == 0)
    def _():
        pool_ext[0:POOL_CARRY, :] = jnp.zeros((POOL_CARRY, G), F32)
        sconv_ext[0:SCONV_CARRY, :] = jnp.zeros((SCONV_CARRY, G), F32)
        dconv_ext[0:DCONV_CARRY, :] = jnp.zeros((DCONV_CARRY, G), F32)

    @pl.when(j > 0)
    def _():
        pool_ext[0:POOL_CARRY, :] = pool_ext[tm:tm + POOL_CARRY, :]
        sconv_ext[0:SCONV_CARRY, :] = sconv_ext[tm:tm + SCONV_CARRY, :]
        dconv_ext[0:DCONV_CARRY, :] = dconv_ext[tm:tm + DCONV_CARRY, :]

    mod = lambda k: ada_ref[k, pl.ds(b, 1), :]
    n_sub = tm // sub_rows
    rows = lambda h: slice(h * sub_rows, (h + 1) * sub_rows)
    front = lambda h: _mixer_front(x_ref[rows(h), :], mod, win_ref)
    proj = front(0)
    for h in range(n_sub):
        proj_next = front(h + 1) if h + 1 < n_sub else None
        mixn = _mixer_prompt_mid(
            proj, j * tm + h * sub_rows, h * sub_rows, h == n_sub - 1,
            sguw_ref, sgub_ref, sgug_ref, sgulb_ref, poolw_ref, pools_ref, sconvw_ref,
            dconvw_ref, dconvb_ref, cvg_ref, cvb_ref, onorm_ref,
            sguv_ref, poolst_ref, sconvst_ref, dconvst_ref, pool_ext, sconv_ext, dconv_ext, layer=layer)
        o_ref[rows(h), :] = _mixer_back(x_ref[rows(h), :], mixn, mod, wout_ref, lng_ref, lnb_ref, layer)
        proj = proj_next


def _mixer_prompt_call(x2d, ada, w_in, w_out, sgu_w, sgu_bias, small, *, layer):
    (sgu_ln_g, sgu_ln_b, pool_bd, pool_scale, sconv_w, dconv_w, dconv_b, conv_ln_g, conv_ln_b,
     out_norm_g, post_g, post_b) = small
    tm = TM_MIX
    nt = SEQ // tm
    x_spec = pl.BlockSpec((tm, D_MODEL), lambda b, j: (b * nt + j, 0))
    st_spec = lambda n: pl.BlockSpec((None, n, G), lambda b, j: (b, 0, 0))
    full2 = lambda a: _const_spec(a.shape, (0,) * a.ndim)
    return pl.pallas_call(
        functools.partial(_mixer_prompt_kernel, layer=layer, sub_rows=SUB_MIX),
        grid=(BATCH, nt),
        in_specs=[
            x_spec,
            _const_spec((None, N_ADA, BATCH, D_MODEL), (layer, 0, DEC_BATCH // BATCH, 0)),
            _const_spec((D_MODEL, IN_COLS), (0, 0)),
            _const_spec((D_MODEL, D_MODEL), (0, 0)),
            _const_spec((None, A_HEADS, CHUNK, CHUNK), (layer, 0, 0, 0)),
            _const_spec((None, CHUNK, G), (layer, 0, 0)),
            full2(sgu_ln_g), full2(sgu_ln_b),
            _const_spec((None, G, G), (layer, 0, 0)),
            full2(pool_scale), full2(sconv_w), full2(dconv_w), full2(dconv_b),
            full2(conv_ln_g), full2(conv_ln_b), full2(out_norm_g), full2(post_g), full2(post_b),
        ],
        out_specs=[x_spec, st_spec(CHUNK), st_spec(POOL_BUF), st_spec(SCONV_W - 1), st_spec(DCONV_W - 1)],
        out_shape=[
            jax.ShapeDtypeStruct((BATCH * SEQ, D_MODEL), F32),
            jax.ShapeDtypeStruct((BATCH, CHUNK, G), F32),
            jax.ShapeDtypeStruct((BATCH, POOL_BUF, G), F32),
            jax.ShapeDtypeStruct((BATCH, SCONV_W - 1, G), F32),
            jax.ShapeDtypeStruct((BATCH, DCONV_W - 1, G), F32),
        ],
        scratch_shapes=[
            pltpu.VMEM((tm + POOL_CARRY, G), F32),
            pltpu.VMEM((tm + SCONV_CARRY, G), F32),
            pltpu.VMEM((tm + DCONV_CARRY, G), F32),
        ],
        compiler_params=pltpu.CompilerParams(
            dimension_semantics=("arbitrary", "arbitrary"), vmem_limit_bytes=VMEM_LIMIT_BYTES),
        name=f"mixer_p{layer}",
    )(x2d, ada, w_in, w_out, sgu_w, sgu_bias, sgu_ln_g, sgu_ln_b, pool_bd, pool_scale, sconv_w,
      dconv_w, dconv_b, conv_ln_g, conv_ln_b, out_norm_g, post_g, post_b)


def _mixer_sample_kernel(x_ref, ada_ref, win_ref, wout_ref, sguw_ref, sgub_ref, sgug_ref, sgulb_ref,
                         poolw_ref, pools_ref, sconvw_ref, dconvw_ref, dconvb_ref, cvg_ref, cvb_ref,
                         onorm_ref, lng_ref, lnb_ref, poolin_ref, sconvin_ref, dconvin_ref,
                         o_ref, sguv_ref, poolst_ref, sconvst_ref, dconvst_ref, *, layer):
    nb = DEC_BATCH
    lrow = pl.ds(layer, 1)
    x = x_ref[...]
    mod = lambda k: ada_ref[k]
    proj = _mixer_front(x, mod, win_ref)
    col = lambda i: proj[:, i * G:(i + 1) * G]
    slab = lambda a, t: a[t * nb:(t + 1) * nb, :]

    u = _gelu(col(0))
    v = _ln(_gelu(col(1)), sgug_ref[lrow, :], sgulb_ref[lrow, :])
    for t in range(DEC_SEQ):
        sguv_ref[t] = slab(v, t)
    mixed = []
    for t in range(DEC_SEQ):
        m = jnp.zeros((nb, G), F32) + sgub_ref[pl.ds(t, 1), :]
        for s in range(t + 1):
            m = m + sguw_ref[t, pl.ds(s, 1), :] * slab(v, s)
        mixed.append(m)
    ya = u * jnp.concatenate(mixed, axis=0)

    xb = col(2)
    e = [poolin_ref[i] for i in range(POOL_BUF)] + [slab(xb, t) for t in range(DEC_SEQ)]
    n_e = len(e)
    s2 = [e[i] + e[i - 1] if i >= 1 else None for i in range(n_e)]
    s4 = [s2[i] + s2[i - 2] if i >= 3 else None for i in range(n_e)]
    s8 = [s4[i] + s4[i - 4] if i >= 7 else None for i in range(n_e)]
    s16 = [s8[i] + s8[i - 8] if i >= 15 else None for i in range(n_e)]
    grp, win = _pool_lane_consts()
    cnt = jnp.minimum(win, float(PAST_LEN + 1))
    sel = jnp.concatenate([_pool_select(grp, s2[i], s4[i], s8[i], s16[i])
                           for i in range(POOL_BUF, n_e)], axis=0)
    pooled = (sel / cnt - xb).astype(BF16)
    yb = _dot(pooled, poolw_ref[...]) * pools_ref[lrow, :]
    for i in range(POOL_BUF):
        poolst_ref[i] = e[n_e - POOL_BUF + i]

    zc = col(4) * col(5)
    es = [sconvin_ref[i] for i in range(SCONV_W - 1)] + [slab(zc, t) for t in range(DEC_SEQ)]
    yconv = jnp.concatenate(
        [sum(sconvw_ref[layer, pl.ds(k, 1), :] * es[t + k] for k in range(SCONV_W))
         for t in range(DEC_SEQ)], axis=0)
    yc = col(3) * yconv
    for i in range(SCONV_W - 1):
        sconvst_ref[i] = es[len(es) - (SCONV_W - 1) + i]

    zd = col(6) * jax.nn.sigmoid(col(7))
    for i in range(DCONV_W - 1 - DEC_SEQ):
        dconvst_ref[i] = dconvin_ref[i + DEC_SEQ]
    for t in range(DEC_SEQ):
        dconvst_ref[DCONV_W - 1 - DEC_SEQ + t] = slab(zd, t)
    conv = []
    for t in range(DEC_SEQ):
        a = jnp.zeros((nb, G), F32)
        for k in range(DCONV_W):
            i = t + k
            src = dconvin_ref[i] if i < DCONV_W - 1 else slab(zd, i - (DCONV_W - 1))
            a = a + dconvw_ref[layer, pl.ds(k, 1), :] * src
        conv.append(a)
    hd = _ln(jnp.concatenate(conv, axis=0) + dconvb_ref[lrow, :], cvg_ref[lrow, :], cvb_ref[lrow, :])
    yd = _silu(hd)

    mixn = _mixer_norm(ya, yb, yc, yd, onorm_ref, layer)
    o_ref[...] = _mixer_back(x, mixn, mod, wout_ref, lng_ref, lnb_ref, layer)


def _mixer_sample_call(x2d, ada, w_in, w_out, sgu_wsm, sgu_bias, small, pool_in, sconv_in, dconv_in,
                       *, layer):
    (sgu_ln_g, sgu_ln_b, pool_bd, pool_scale, sconv_w, dconv_w, dconv_b, conv_ln_g, conv_ln_b,
     out_norm_g, post_g, post_b) = small
    rows = DEC_SEQ * DEC_BATCH
    full2 = lambda a: _const_spec(a.shape, (0,) * a.ndim)
    st = lambda n: _const_spec((None, n, DEC_BATCH, G), (layer, 0, 0, 0))
    st_out = lambda n: pl.BlockSpec((n, DEC_BATCH, G), lambda i: (0, 0, 0))
    return pl.pallas_call(
        functools.partial(_mixer_sample_kernel, layer=layer),
        grid=(1,),
        in_specs=[
            _const_spec((rows, D_MODEL), (0, 0)),
            _const_spec((None, N_ADA, DEC_BATCH, D_MODEL), (layer, 0, 0, 0)),
            _const_spec((D_MODEL, IN_COLS), (0, 0)),
            _const_spec((D_MODEL, D_MODEL), (0, 0)),
            _const_spec((None, DEC_SEQ, DEC_SEQ, G), (layer, 0, 0, 0)),
            _const_spec((None, DEC_SEQ, G), (layer, 0, 0)),
            full2(sgu_ln_g), full2(sgu_ln_b),
            _const_spec((None, G, G), (layer, 0, 0)),
            full2(pool_scale), full2(sconv_w), full2(dconv_w), full2(dconv_b),
            full2(conv_ln_g), full2(conv_ln_b), full2(out_norm_g), full2(post_g), full2(post_b),
            st(POOL_BUF), st(SCONV_W - 1), st(DCONV_W - 1),
        ],
        out_specs=[pl.BlockSpec((rows, D_MODEL), lambda i: (0, 0)),
                   st_out(DEC_SEQ), st_out(POOL_BUF), st_out(SCONV_W - 1), st_out(DCONV_W - 1)],
        out_shape=[
            jax.ShapeDtypeStruct((rows, D_MODEL), F32),
            jax.ShapeDtypeStruct((DEC_SEQ, DEC_BATCH, G), F32),
            jax.ShapeDtypeStruct((POOL_BUF, DEC_BATCH, G), F32),
            jax.ShapeDtypeStruct((SCONV_W - 1, DEC_BATCH, G), F32),
            jax.ShapeDtypeStruct((DCONV_W - 1, DEC_BATCH, G), F32),
        ],
        compiler_params=pltpu.CompilerParams(
            dimension_semantics=("arbitrary",), vmem_limit_bytes=VMEM_LIMIT_BYTES),
        name=f"mixer_s{layer}",
    )(x2d, ada, w_in, w_out, sgu_wsm, sgu_bias, sgu_ln_g, sgu_ln_b, pool_bd, pool_scale, sconv_w,
      dconv_w, dconv_b, conv_ln_g, conv_ln_b, out_norm_g, post_g, post_b, pool_in, sconv_in, dconv_in)


def kernel(x_prompt, x_sample, state_pool, state_sconv, state_dconv, c_prompt, c_sample, ln_in_g, ln_in_b, w_ada, b_ada, ffn1_w_gate, ffn1_w_up, ffn1_w_down, w_in, sgu_ln_g, sgu_ln_b, sgu_w, sgu_b, pool_w, pool_scale, sconv_w, dconv_w, dconv_b, conv_ln_g, conv_ln_b, out_norm_g, w_out, ffn2_w_gate, ffn2_w_up, ffn2_w_down, post_ln_g, post_ln_b):
    eye = jnp.eye(len(POOL_WINDOWS), dtype=F32)
    pool_bd = jnp.einsum('lgcd,gh->lgchd', pool_w, eye).reshape(DEPTH, G, G).astype(BF16)
    sgu_bias = jnp.repeat(jnp.transpose(sgu_b, (0, 2, 1)), HEAD_DIM, axis=-1)
    sgu_wsm = jnp.repeat(jnp.transpose(sgu_w[:, :, :DEC_SEQ, :DEC_SEQ], (0, 2, 3, 1)),
                         HEAD_DIM, axis=-1)
    in_g, in_b = ln_in_g.reshape(1, D_MODEL), ln_in_b.reshape(1, D_MODEL)
    small = (sgu_ln_g, sgu_ln_b, pool_bd, pool_scale, sconv_w, dconv_w, dconv_b, conv_ln_g,
             conv_ln_b, out_norm_g, post_ln_g, post_ln_b)

    ada = _ada_call(jnp.concatenate([c_sample, c_prompt], axis=0), w_ada, b_ada)

    xs = jnp.transpose(x_sample, (1, 0, 2)).reshape(DEC_SEQ * DEC_BATCH, D_MODEL)
    pool_in = jnp.transpose(state_pool, (0, 2, 1, 3))
    sconv_in = jnp.transpose(state_sconv, (0, 2, 1, 3))
    dconv_in = jnp.transpose(state_dconv, (0, 2, 1, 3))
    xp = x_prompt.reshape(BATCH * SEQ, D_MODEL)

    ffn1 = [w[0].astype(BF16) for w in (ffn1_w_gate, ffn1_w_up, ffn1_w_down)]
    ffn = functools.partial(_ffn_call, post_g=post_ln_g, post_b=post_ln_b, in_g=in_g, in_b=in_b)
    outs_p, outs_s = [], []
    for l in range(DEPTH):
        last = l == DEPTH - 1
        side = [(w_in, l), (w_out, l), (ffn2_w_gate, l), (ffn2_w_up, l), (ffn2_w_down, l)]
        xp, (w_in_b, w_out_b, *ffn2) = ffn(xp, ada, *ffn1, layer=l, sub=0, first=(l == 0), prompt=True,
                                           side=side)
        xs, _ = ffn(xs, ada, *ffn1, layer=l, sub=0, first=(l == 0), prompt=False)
        xp, *st_p = _mixer_prompt_call(xp, ada, w_in_b, w_out_b, sgu_w, sgu_bias, small, layer=l)
        xs, *st_s = _mixer_sample_call(xs, ada, w_in_b, w_out_b, sgu_wsm, sgu_bias, small,
                                       pool_in, sconv_in, dconv_in, layer=l)
        side = [] if last else [(ffn1_w_gate, l + 1), (ffn1_w_up, l + 1), (ffn1_w_down, l + 1)]
        xp, ffn1_next = ffn(xp, ada, *ffn2, layer=l, sub=2, first=False, prompt=True, side=side)
        xs, _ = ffn(xs, ada, *ffn2, layer=l, sub=2, first=False, prompt=False)
        ffn1 = ffn1_next
        outs_p.append(st_p)
        outs_s.append(st_s)

    y_prompt = xp.reshape(BATCH, SEQ, D_MODEL)
    y_sample = jnp.transpose(xs.reshape(DEC_SEQ, DEC_BATCH, D_MODEL), (1, 0, 2))
    stack_p = lambda i: jnp.stack([o[i] for o in outs_p], axis=0)
    stack_s = lambda i: jnp.transpose(jnp.stack([o[i] for o in outs_s], axis=0), (0, 2, 1, 3))
    return (y_prompt, y_sample, stack_p(0), stack_s(0), stack_p(1), stack_s(1),
            stack_p(2), stack_s(2), stack_p(3), stack_s(3))
```

```python
import functools

import jax
import jax.numpy as jnp
from jax import lax
from jax.experimental import pallas as pl
from jax.experimental.pallas import tpu as pltpu

D_MODEL = 1024
BATCH = 8
SEQ = 2048
DEPTH = 2
DEC_BATCH = 128
DEC_SEQ = 8
PAST_LEN = 16384
G = 256
HEAD_DIM = 64
A_HEADS = G // HEAD_DIM
CHUNK = 128
POOL_WINDOWS = (2, 4, 8, 16)
POOL_GDIM = G // len(POOL_WINDOWS)
POOL_BUF = 15
SCONV_W = 3
DCONV_W = 31
D_FF = 2816
N_ADA = 9
ALPHA = (2 * DEPTH) ** 0.25
IN_COLS = 8 * G
N_SEQ_ALL = DEC_BATCH + BATCH
ADA_PER_STEP = 3

SUBLANES = 8
BF16_SUBLANES = 16
POOL_CARRY = 16
SCONV_CARRY = 8
DCONV_CARRY = 32
VMEM_LIMIT_BYTES = 56 * 1024 * 1024

TM_FFN = 512
TM_MIX = 512
SUB_MIX = 256
SAMPLE_BLK = 32

BF16 = jnp.bfloat16
F32 = jnp.float32


def _ln(x, g, b, eps=1e-5):
    mu = jnp.mean(x, axis=-1, keepdims=True)
    xc = x - mu
    var = jnp.mean(xc * xc, axis=-1, keepdims=True)
    return xc * lax.rsqrt(var + eps) * g + b


def _gelu(x):
    return 0.5 * x * (1.0 + lax.erf(x * 0.7071067811865476))


def _silu(x):
    return x * jax.nn.sigmoid(x)


def _rows(x, m):
    if m.shape[0] == 1 or m.shape[0] == x.shape[0]:
        return m
    reps = x.shape[0] // m.shape[0]
    return jnp.broadcast_to(m[None], (reps,) + m.shape).reshape(x.shape)


def _dot(a, b):
    return jnp.dot(a, b, preferred_element_type=F32)


def _const_spec(shape, index):
    return pl.BlockSpec(shape, lambda *_: index, pipeline_mode=pl.Buffered(1))


def _ada_kernel(c_ref, w_ref, b_ref, o_ref):
    s = _silu(c_ref[...]).astype(BF16)
    y = _dot(s, w_ref[...].astype(BF16))
    for k in range(ADA_PER_STEP):
        o_ref[k] = y[:, k * D_MODEL:(k + 1) * D_MODEL] + b_ref[k]


def _ada_call(c_all, w_ada, b_ada):
    return pl.pallas_call(
        _ada_kernel,
        grid=(DEPTH, N_ADA // ADA_PER_STEP),
        in_specs=[
            pl.BlockSpec((N_SEQ_ALL, D_MODEL), lambda l, k: (0, 0)),
            pl.BlockSpec((None, D_MODEL, ADA_PER_STEP * D_MODEL), lambda l, k: (l, 0, k)),
            pl.BlockSpec((None, ADA_PER_STEP, 1, D_MODEL), lambda l, k: (l, k, 0, 0)),
        ],
        out_specs=pl.BlockSpec((None, ADA_PER_STEP, N_SEQ_ALL, D_MODEL), lambda l, k: (l, k, 0, 0)),
        out_shape=jax.ShapeDtypeStruct((DEPTH, N_ADA, N_SEQ_ALL, D_MODEL), F32),
        compiler_params=pltpu.CompilerParams(
            dimension_semantics=("arbitrary", "arbitrary"), vmem_limit_bytes=VMEM_LIMIT_BYTES),
        name="ada",
    )(c_all, w_ada, b_ada.reshape(DEPTH, N_ADA, 1, D_MODEL))


def _ffn_math(x, mod, wg_ref, wu_ref, wd_ref, lng_ref, lnb_ref, *, layer, sub):
    sh, sc, gt = mod(0), mod(1), mod(2)
    xm = (x * (1.0 + _rows(x, sc)) + _rows(x, sh)).astype(BF16)
    hg = _dot(xm, wg_ref[...])
    hu = _dot(xm, wu_ref[...])
    h = (_silu(hg) * hu).astype(BF16)
    gain = 0.5 * (1.0 + _rows(x, gt))
    half = x.shape[0] // 2
    outs = []
    for r in (slice(0, half), slice(half, 2 * half)):
        y = ALPHA * x[r] + (gain if gain.shape[0] == 1 else gain[r]) * _dot(h[r], wd_ref[...])
        outs.append(_ln(y, lng_ref[layer, pl.ds(sub, 1), :], lnb_ref[layer, pl.ds(sub, 1), :]))
    return jnp.concatenate(outs, axis=0)


def _ffn_kernel(xp_ref, xs_ref, adap_ref, adas_ref, wg_ref, wu_ref, wd_ref, lng_ref, lnb_ref,
                ing_ref, inb_ref, *rest, layer, sub, first, n_side, n_prompt, tiles_per_seq):
    side_in, (op_ref, os_ref), side_out = rest[:n_side], rest[n_side:n_side + 2], rest[n_side + 2:]
    s = pl.program_id(0)

    def half_step(x_ref, o_ref, mod):
        x = x_ref[...]
        if first:
            x = _ln(x, ing_ref[...], inb_ref[...])
        o_ref[...] = _ffn_math(x, mod, wg_ref, wu_ref, wd_ref, lng_ref, lnb_ref, layer=layer, sub=sub)

    @pl.when(s < n_prompt)
    def _():
        b = lax.div(s, tiles_per_seq)
        half_step(xp_ref, op_ref, lambda k: adap_ref[k, pl.ds(b, 1), :])
        for src, dst in zip(side_in, side_out):
            dst[...] = src[...].astype(BF16)

    @pl.when(s >= n_prompt)
    def _():
        r0 = pl.multiple_of((s - n_prompt) * SAMPLE_BLK, SAMPLE_BLK)
        half_step(xs_ref, os_ref, lambda k: adas_ref[k, pl.ds(r0, SAMPLE_BLK), :])


def _ffn_call(xp, xs, ada, wg, wu, wd, post_g, post_b, in_g, in_b, *, layer, sub, first, side=()):
    nt = SEQ // TM_FFN
    n_prompt = BATCH * nt
    n_sample = DEC_BATCH // SAMPLE_BLK
    tile_p = lambda s: jnp.minimum(s, n_prompt - 1)
    tile_s = lambda s: jnp.clip(s - n_prompt, 0, n_sample - 1)
    xp_spec = pl.BlockSpec((TM_FFN, D_MODEL), lambda s: (tile_p(s), 0))
    xs_spec = pl.BlockSpec((DEC_SEQ * SAMPLE_BLK, D_MODEL), lambda s: (tile_s(s), 0))
    k_blk = (0 if sub == 0 else 6) // 3
    side_in_specs, side_out_specs, side_shapes = [], [], []
    for w, wl in side:
        k_dim, n_dim = w.shape[1:]
        steps = n_prompt
        while (k_dim // steps) % BF16_SUBLANES:
            steps //= 2
        chunk = lambda s, per=n_prompt // steps: tile_p(s) // per
        side_in_specs.append(pl.BlockSpec((None, k_dim // steps, n_dim),
                                          lambda s, wl=wl, chunk=chunk: (wl, chunk(s), 0)))
        side_out_specs.append(pl.BlockSpec((k_dim // steps, n_dim), lambda s, chunk=chunk: (chunk(s), 0)))
        side_shapes.append(jax.ShapeDtypeStruct((k_dim, n_dim), BF16))
    kern = functools.partial(_ffn_kernel, layer=layer, sub=sub, first=first, n_side=len(side),
                             n_prompt=n_prompt, tiles_per_seq=nt)
    res = pl.pallas_call(
        kern,
        grid=(n_prompt + n_sample,),
        in_specs=[
            xp_spec, xs_spec,
            _const_spec((None, 3, BATCH, D_MODEL), (layer, k_blk, DEC_BATCH // BATCH, 0)),
            _const_spec((None, 3, DEC_BATCH, D_MODEL), (layer, k_blk, 0, 0)),
            _const_spec((D_MODEL, D_FF), (0, 0)),
            _const_spec((D_MODEL, D_FF), (0, 0)),
            _const_spec((D_FF, D_MODEL), (0, 0)),
            _const_spec((DEPTH, 3, D_MODEL), (0, 0, 0)),
            _const_spec((DEPTH, 3, D_MODEL), (0, 0, 0)),
            _const_spec((1, D_MODEL), (0, 0)),
            _const_spec((1, D_MODEL), (0, 0)),
        ] + side_in_specs,
        out_specs=[xp_spec, xs_spec] + side_out_specs,
        out_shape=[jax.ShapeDtypeStruct(xp.shape, F32), jax.ShapeDtypeStruct(xs.shape, F32)] + side_shapes,
        compiler_params=pltpu.CompilerParams(
            dimension_semantics=("arbitrary",), vmem_limit_bytes=VMEM_LIMIT_BYTES),
        name=f"ffn{sub}_{layer}",
    )(xp, xs, ada, ada, wg, wu, wd, post_g, post_b, in_g, in_b, *[w for w, _ in side])
    return res[0], res[1], list(res[2:])


def _pool_lane_consts():
    lane = lax.broadcasted_iota(jnp.int32, (1, G), 1)
    grp = lane // POOL_GDIM
    win = jnp.where(grp == 0, float(POOL_WINDOWS[0]),
                    jnp.where(grp == 1, float(POOL_WINDOWS[1]),
                              jnp.where(grp == 2, float(POOL_WINDOWS[2]), float(POOL_WINDOWS[3]))))
    return grp, win


def _pool_select(grp, s2, s4, s8, s16):
    return jnp.where(grp == 0, s2, jnp.where(grp == 1, s4, jnp.where(grp == 2, s8, s16)))


def _rms_group(y, g, eps=1e-6):
    return y * lax.rsqrt(jnp.mean(y * y, axis=-1, keepdims=True) + eps) * g


def _mixer_front(x, mod, win_ref):
    sh, sc = mod(0), mod(1)
    xm = (x * (1.0 + _rows(x, sc)) + _rows(x, sh)).astype(BF16)
    return _dot(xm, win_ref[...])


def _mixer_norm(ya, yb, yc, yd, onorm_ref, layer):
    parts = [ya, yb, yc, yd]
    normed = [_rms_group(p, onorm_ref[pl.ds(layer, 1), i * G:(i + 1) * G]).astype(BF16)
              for i, p in enumerate(parts)]
    return jnp.concatenate(normed, axis=-1)


def _mixer_back(x, mixn, mod, wout_ref, lng_ref, lnb_ref, layer):
    y = ALPHA * x + (1.0 + _rows(x, mod(2))) * _dot(mixn, wout_ref[...])
    return _ln(y, lng_ref[layer, pl.ds(1, 1), :], lnb_ref[layer, pl.ds(1, 1), :])


def _mixer_prompt_mid(proj, pos0, r0, last, sguw_ref, sgub_ref, sgug_ref, sgulb_ref,
                      poolw_ref, pools_ref, sconvw_ref, dconvw_ref, dconvb_ref, cvg_ref, cvb_ref,
                      onorm_ref, sguv_ref, poolst_ref, sconvst_ref, dconvst_ref,
                      pool_ext, sconv_ext, dconv_ext, *, layer):
    n = proj.shape[0]
    lrow = pl.ds(layer, 1)
    col = lambda i: proj[:, i * G:(i + 1) * G]

    u = _gelu(col(0))
    v = _ln(_gelu(col(1)), sgug_ref[lrow, :], sgulb_ref[lrow, :])
    if last:
        sguv_ref[...] = v[n - CHUNK:, :]
    r_i = lax.broadcasted_iota(jnp.int32, (CHUNK, CHUNK), 0)
    c_i = lax.broadcasted_iota(jnp.int32, (CHUNK, CHUNK), 1)
    wcat = jnp.concatenate([jnp.where(r_i >= c_i, sguw_ref[h], 0.0) for h in range(A_HEADS)],
                           axis=1).astype(BF16)
    head = lax.broadcasted_iota(jnp.int32, (1, G), 1) // HEAD_DIM
    bias = sgub_ref[...]
    mixed = []
    for c in range(n // CHUNK):
        vc = v[c * CHUNK:(c + 1) * CHUNK, :]
        rhs = jnp.concatenate([jnp.where(head == h, vc, 0.0) for h in range(A_HEADS)],
                              axis=0).astype(BF16)
        mixed.append(_dot(wcat, rhs) + bias)
    ya = u * jnp.concatenate(mixed, axis=0)

    xb = col(2)
    pool_ext[r0 + POOL_CARRY:r0 + POOL_CARRY + n, :] = xb
    e = pool_ext[r0:r0 + POOL_CARRY + n, :]
    s2 = e + pltpu.roll(e, 1, 0)
    s4 = s2 + pltpu.roll(s2, 2, 0)
    s8 = s4 + pltpu.roll(s4, 4, 0)
    s16 = s8 + pltpu.roll(s8, 8, 0)
    grp, win = _pool_lane_consts()
    ssel = _pool_select(grp, s2, s4, s8, s16)[POOL_CARRY:, :]
    pos = pos0 + lax.broadcasted_iota(jnp.int32, (n, 1), 0)
    cnt = jnp.minimum(win, (pos + 1).astype(F32))
    pooled = (ssel / cnt - xb).astype(BF16)
    yb = _dot(pooled, poolw_ref[...]) * pools_ref[lrow, :]
    if last:
        poolst_ref[...] = pool_ext[r0 + n + POOL_CARRY - POOL_BUF:r0 + n + POOL_CARRY, :]

    zc = col(4) * col(5)
    sconv_ext[r0 + SCONV_CARRY:r0 + SCONV_CARRY + n, :] = zc
    es = sconv_ext[r0:r0 + SCONV_CARRY + n, :]
    yconv = (sconvw_ref[layer, pl.ds(2, 1), :] * es
             + sconvw_ref[layer, pl.ds(1, 1), :] * pltpu.roll(es, 1, 0)
             + sconvw_ref[layer, pl.ds(0, 1), :] * pltpu.roll(es, 2, 0))
    yc = col(3) * yconv[SCONV_CARRY:, :]
    if last:
        sconvst_ref[...] = sconv_ext[r0 + n + SCONV_CARRY - (SCONV_W - 1):r0 + n + SCONV_CARRY, :]

    zd = col(6) * jax.nn.sigmoid(col(7))
    dconv_ext[r0 + DCONV_CARRY:r0 + DCONV_CARRY + n, :] = zd
    if last:
        dconvst_ref[...] = dconv_ext[r0 + n + DCONV_CARRY - (DCONV_W - 1):r0 + n + DCONV_CARRY, :]
    ed = dconv_ext[r0:r0 + DCONV_CARRY + n, :]
    acc = jnp.zeros((n, G), F32)
    for r in range(SUBLANES):
        er = ed if r == 0 else pltpu.roll(ed, r, 0)
        for q in range(DCONV_CARRY // SUBLANES):
            tap = SUBLANES * q + r
            if tap >= DCONV_W:
                continue
            lo = DCONV_CARRY - SUBLANES * q
            acc = acc + dconvw_ref[layer, pl.ds(DCONV_W - 1 - tap, 1), :] * er[lo:lo + n, :]
    hd = _ln(acc + dconvb_ref[lrow, :], cvg_ref[lrow, :], cvb_ref[lrow, :])
    yd = _silu(hd)

    return _mixer_norm(ya, yb, yc, yd, onorm_ref, layer)


def _mixer_prompt_step(b, j, x_ref, ada_ref, win_ref, wout_ref, sguw_ref, sgub_ref, sgug_ref, sgulb_ref,
                       poolw_ref, pools_ref, sconvw_ref, dconvw_ref, dconvb_ref, cvg_ref, cvb_ref,
                       onorm_ref, lng_ref, lnb_ref,
                       o_ref, sguv_ref, poolst_ref, sconvst_ref, dconvst_ref,
                       pool_ext, sconv_ext, dconv_ext, *, layer, sub_rows):
    tm = x_ref.shape[0]

    @pl.when(j == 0)
    def _():
        pool_ext[0:POOL_CARRY, :] = jnp.zeros((POOL_CARRY, G), F32)
        sconv_ext[0:SCONV_CARRY, :] = jnp.zeros((SCONV_CARRY, G), F32)
        dconv_ext[0:DCONV_CARRY, :] = jnp.zeros((DCONV_CARRY, G), F32)

    @pl.when(j > 0)
    def _():
        pool_ext[0:POOL_CARRY, :] = pool_ext[tm:tm + POOL_CARRY, :]
        sconv_ext[0:SCONV_CARRY, :] = sconv_ext[tm:tm + SCONV_CARRY, :]
        dconv_ext[0:DCONV_CARRY, :] = dconv_ext[tm:tm + DCONV_CARRY, :]

    mod = lambda k: ada_ref[k, pl.ds(b, 1), :]
    n_sub = tm // sub_rows
    rows = lambda h: slice(h * sub_rows, (h + 1) * sub_rows)
    front = lambda h: _mixer_front(x_ref[rows(h), :], mod, win_ref)
    proj = front(0)
    for h in range(n_sub):
        proj_next = front(h + 1) if h + 1 < n_sub else None
        mixn = _mixer_prompt_mid(
            proj, j * tm + h * sub_rows, h * sub_rows, h == n_sub - 1,
            sguw_ref, sgub_ref, sgug_ref, sgulb_ref, poolw_ref, pools_ref, sconvw_ref,
            dconvw_ref, dconvb_ref, cvg_ref, cvb_ref, onorm_ref,
            sguv_ref, poolst_ref, sconvst_ref, dconvst_ref, pool_ext, sconv_ext, dconv_ext, layer=layer)
        o_ref[rows(h), :] = _mixer_back(x_ref[rows(h), :], mixn, mod, wout_ref, lng_ref, lnb_ref, layer)
        proj = proj_next


def _mixer_sample_step(mod, x_ref, win_ref, wout_ref, sguw_ref, sgub_ref, sgug_ref, sgulb_ref,
                       poolw_ref, pools_ref, sconvw_ref, dconvw_ref, dconvb_ref, cvg_ref, cvb_ref,
                       onorm_ref, lng_ref, lnb_ref, poolin_ref, sconvin_ref, dconvin_ref,
                       o_ref, sguv_ref, poolst_ref, sconvst_ref, dconvst_ref, *, layer):
    nb = SAMPLE_BLK
    lrow = pl.ds(layer, 1)
    x = x_ref[...]
    proj = _mixer_front(x, mod, win_ref)
    col = lambda i: proj[:, i * G:(i + 1) * G]
    slab = lambda a, t: a[t * nb:(t + 1) * nb, :]

    u = _gelu(col(0))
    v = _ln(_gelu(col(1)), sgug_ref[lrow, :], sgulb_ref[lrow, :])
    for t in range(DEC_SEQ):
        sguv_ref[t] = slab(v, t)
    mixed = []
    for t in range(DEC_SEQ):
        m = jnp.zeros((nb, G), F32) + sgub_ref[pl.ds(t, 1), :]
        for s in range(t + 1):
            m = m + sguw_ref[t, pl.ds(s, 1), :] * slab(v, s)
        mixed.append(m)
    ya = u * jnp.concatenate(mixed, axis=0)

    xb = col(2)
    e = [poolin_ref[i] for i in range(POOL_BUF)] + [slab(xb, t) for t in range(DEC_SEQ)]
    n_e = len(e)
    s2 = [e[i] + e[i - 1] if i >= 1 else None for i in range(n_e)]
    s4 = [s2[i] + s2[i - 2] if i >= 3 else None for i in range(n_e)]
    s8 = [s4[i] + s4[i - 4] if i >= 7 else None for i in range(n_e)]
    s16 = [s8[i] + s8[i - 8] if i >= 15 else None for i in range(n_e)]
    grp, win = _pool_lane_consts()
    cnt = jnp.minimum(win, float(PAST_LEN + 1))
    sel = jnp.concatenate([_pool_select(grp, s2[i], s4[i], s8[i], s16[i])
                           for i in range(POOL_BUF, n_e)], axis=0)
    pooled = (sel / cnt - xb).astype(BF16)
    yb = _dot(pooled, poolw_ref[...]) * pools_ref[lrow, :]
    for i in range(POOL_BUF):
        poolst_ref[i] = e[n_e - POOL_BUF + i]

    zc = col(4) * col(5)
    es = [sconvin_ref[i] for i in range(SCONV_W - 1)] + [slab(zc, t) for t in range(DEC_SEQ)]
    yconv = jnp.concatenate(
        [sum(sconvw_ref[layer, pl.ds(k, 1), :] * es[t + k] for k in range(SCONV_W))
         for t in range(DEC_SEQ)], axis=0)
    yc = col(3) * yconv
    for i in range(SCONV_W - 1):
        sconvst_ref[i] = es[len(es) - (SCONV_W - 1) + i]

    zd = col(6) * jax.nn.sigmoid(col(7))
    for i in range(DCONV_W - 1 - DEC_SEQ):
        dconvst_ref[i] = dconvin_ref[i + DEC_SEQ]
    for t in range(DEC_SEQ):
        dconvst_ref[DCONV_W - 1 - DEC_SEQ + t] = slab(zd, t)
    conv = []
    for t in range(DEC_SEQ):
        a = jnp.zeros((nb, G), F32)
        for k in range(DCONV_W):
            i = t + k
            src = dconvin_ref[i] if i < DCONV_W - 1 else slab(zd, i - (DCONV_W - 1))
            a = a + dconvw_ref[layer, pl.ds(k, 1), :] * src
        conv.append(a)
    hd = _ln(jnp.concatenate(conv, axis=0) + dconvb_ref[lrow, :], cvg_ref[lrow, :], cvb_ref[lrow, :])
    yd = _silu(hd)

    mixn = _mixer_norm(ya, yb, yc, yd, onorm_ref, layer)
    o_ref[...] = _mixer_back(x, mixn, mod, wout_ref, lng_ref, lnb_ref, layer)


def _mixer_kernel(xp_ref, xs_ref, adap_ref, adas_ref, win_ref, wout_ref, sguw_ref, sguwsm_ref, sgub_ref,
                  sgug_ref, sgulb_ref, poolw_ref, pools_ref, sconvw_ref, dconvw_ref, dconvb_ref,
                  cvg_ref, cvb_ref, onorm_ref, lng_ref, lnb_ref, poolin_ref, sconvin_ref, dconvin_ref,
                  op_ref, sguvp_ref, poolp_ref, sconvp_ref, dconvp_ref,
                  os_ref, sguvs_ref, pools_out_ref, sconvs_ref, dconvs_ref,
                  pool_ext, sconv_ext, dconv_ext, *, layer, n_prompt, tiles_per_seq):
    s = pl.program_id(0)
    shared = (sgug_ref, sgulb_ref, poolw_ref, pools_ref, sconvw_ref, dconvw_ref, dconvb_ref,
              cvg_ref, cvb_ref, onorm_ref, lng_ref, lnb_ref)

    @pl.when(s < n_prompt)
    def _():
        _mixer_prompt_step(
            lax.div(s, tiles_per_seq), lax.rem(s, tiles_per_seq), xp_ref, adap_ref, win_ref, wout_ref,
            sguw_ref, sgub_ref, *shared, op_ref, sguvp_ref, poolp_ref, sconvp_ref, dconvp_ref,
            pool_ext, sconv_ext, dconv_ext, layer=layer, sub_rows=SUB_MIX)

    @pl.when(s >= n_prompt)
    def _():
        r0 = pl.multiple_of((s - n_prompt) * SAMPLE_BLK, SAMPLE_BLK)
        _mixer_sample_step(
            lambda k: adas_ref[k, pl.ds(r0, SAMPLE_BLK), :], xs_ref, win_ref, wout_ref,
            sguwsm_ref, sgub_ref, *shared, poolin_ref, sconvin_ref, dconvin_ref,
            os_ref, sguvs_ref, pools_out_ref, sconvs_ref, dconvs_ref, layer=layer)


def _mixer_call(xp, xs, ada, w_in, w_out, sgu_w, sgu_wsm, sgu_bias, small, pool_in, sconv_in, dconv_in,
                *, layer):
    (sgu_ln_g, sgu_ln_b, pool_bd, pool_scale, sconv_w, dconv_w, dconv_b, conv_ln_g, conv_ln_b,
     out_norm_g, post_g, post_b) = small
    tm = TM_MIX
    nt = SEQ // tm
    n_prompt = BATCH * nt
    n_sample = DEC_BATCH // SAMPLE_BLK
    tile_p = lambda s: jnp.minimum(s, n_prompt - 1)
    tile_s = lambda s: jnp.clip(s - n_prompt, 0, n_sample - 1)
    xp_spec = pl.BlockSpec((tm, D_MODEL), lambda s: (tile_p(s), 0))
    xs_spec = pl.BlockSpec((DEC_SEQ * SAMPLE_BLK, D_MODEL), lambda s: (tile_s(s), 0))
    stp_spec = lambda n: pl.BlockSpec((None, n, G), lambda s: (tile_p(s) // nt, 0, 0))
    sts_in = lambda n: pl.BlockSpec((None, None, n, SAMPLE_BLK, G), lambda s: (layer, tile_s(s), 0, 0, 0))
    sts_out = lambda n: pl.BlockSpec((None, n, SAMPLE_BLK, G), lambda s: (tile_s(s), 0, 0, 0))
    full2 = lambda a: _const_spec(a.shape, (0,) * a.ndim)
    state_rows = (CHUNK, POOL_BUF, SCONV_W - 1, DCONV_W - 1)
    sample_rows = (DEC_SEQ, POOL_BUF, SCONV_W - 1, DCONV_W - 1)
    return pl.pallas_call(
        functools.partial(_mixer_kernel, layer=layer, n_prompt=n_prompt, tiles_per_seq=nt),
        grid=(n_prompt + n_sample,),
        in_specs=[
            xp_spec, xs_spec,
            _const_spec((None, 3, BATCH, D_MODEL), (layer, 1, DEC_BATCH // BATCH, 0)),
            _const_spec((None, 3, DEC_BATCH, D_MODEL), (layer, 1, 0, 0)),
            _const_spec((D_MODEL, IN_COLS), (0, 0)),
            _const_spec((D_MODEL, D_MODEL), (0, 0)),
            _const_spec((None, A_HEADS, CHUNK, CHUNK), (layer, 0, 0, 0)),
            _const_spec((None, DEC_SEQ, DEC_SEQ, G), (layer, 0, 0, 0)),
            _const_spec((None, CHUNK, G), (layer, 0, 0)),
            full2(sgu_ln_g), full2(sgu_ln_b),
            _const_spec((None, G, G), (layer, 0, 0)),
            full2(pool_scale), full2(sconv_w), full2(dconv_w), full2(dconv_b),
            full2(conv_ln_g), full2(conv_ln_b), full2(out_norm_g), full2(post_g), full2(post_b),
            sts_in(POOL_BUF), sts_in(SCONV_W - 1), sts_in(DCONV_W - 1),
        ],
        out_specs=([xp_spec] + [stp_spec(n) for n in state_rows]
                   + [xs_spec] + [sts_out(n) for n in sample_rows]),
        out_shape=([jax.ShapeDtypeStruct(xp.shape, F32)]
                   + [jax.ShapeDtypeStruct((BATCH, n, G), F32) for n in state_rows]
                   + [jax.ShapeDtypeStruct(xs.shape, F32)]
                   + [jax.ShapeDtypeStruct((n_sample, n, SAMPLE_BLK, G), F32) for n in sample_rows]),
        scratch_shapes=[
            pltpu.VMEM((tm + POOL_CARRY, G), F32),
            pltpu.VMEM((tm + SCONV_CARRY, G), F32),
            pltpu.VMEM((tm + DCONV_CARRY, G), F32),
        ],
        compiler_params=pltpu.CompilerParams(
            dimension_semantics=("arbitrary",), vmem_limit_bytes=VMEM_LIMIT_BYTES),
        name=f"mixer_{layer}",
    )(xp, xs, ada, ada, w_in, w_out, sgu_w, sgu_wsm, sgu_bias, sgu_ln_g, sgu_ln_b, pool_bd, pool_scale,
      sconv_w, dconv_w, dconv_b, conv_ln_g, conv_ln_b, out_norm_g, post_g, post_b,
      pool_in, sconv_in, dconv_in)


def kernel(x_prompt, x_sample, state_pool, state_sconv, state_dconv, c_prompt, c_sample, ln_in_g, ln_in_b, w_ada, b_ada, ffn1_w_gate, ffn1_w_up, ffn1_w_down, w_in, sgu_ln_g, sgu_ln_b, sgu_w, sgu_b, pool_w, pool_scale, sconv_w, dconv_w, dconv_b, conv_ln_g, conv_ln_b, out_norm_g, w_out, ffn2_w_gate, ffn2_w_up, ffn2_w_down, post_ln_g, post_ln_b):
    eye = jnp.eye(len(POOL_WINDOWS), dtype=F32)
    pool_bd = jnp.einsum('lgcd,gh->lgchd', pool_w, eye).reshape(DEPTH, G, G).astype(BF16)
    sgu_bias = jnp.repeat(jnp.transpose(sgu_b, (0, 2, 1)), HEAD_DIM, axis=-1)
    sgu_wsm = jnp.repeat(jnp.transpose(sgu_w[:, :, :DEC_SEQ, :DEC_SEQ], (0, 2, 3, 1)),
                         HEAD_DIM, axis=-1)
    in_g, in_b = ln_in_g.reshape(1, D_MODEL), ln_in_b.reshape(1, D_MODEL)
    small = (sgu_ln_g, sgu_ln_b, pool_bd, pool_scale, sconv_w, dconv_w, dconv_b, conv_ln_g,
             conv_ln_b, out_norm_g, post_ln_g, post_ln_b)

    ada = _ada_call(jnp.concatenate([c_sample, c_prompt], axis=0), w_ada, b_ada)

    n_blk = DEC_BATCH // SAMPLE_BLK
    to_blocks = lambda a: jnp.swapaxes(a.reshape(a.shape[:-3] + (n_blk, SAMPLE_BLK) + a.shape[-2:]), -3, -2)
    from_blocks = lambda a: jnp.swapaxes(a, -3, -2).reshape(a.shape[:-4] + (DEC_BATCH,) + a.shape[-3:-2]
                                                           + a.shape[-1:])
    xs = to_blocks(x_sample).reshape(DEC_SEQ * DEC_BATCH, D_MODEL)
    pool_in, sconv_in, dconv_in = to_blocks(state_pool), to_blocks(state_sconv), to_blocks(state_dconv)
    xp = x_prompt.reshape(BATCH * SEQ, D_MODEL)

    ffn1 = [w[0].astype(BF16) for w in (ffn1_w_gate, ffn1_w_up, ffn1_w_down)]
    ffn = functools.partial(_ffn_call, post_g=post_ln_g, post_b=post_ln_b, in_g=in_g, in_b=in_b)
    outs_p, outs_s = [], []
    for l in range(DEPTH):
        side = [(w_in, l), (w_out, l), (ffn2_w_gate, l), (ffn2_w_up, l), (ffn2_w_down, l)]
        xp, xs, (w_in_b, w_out_b, *ffn2) = ffn(xp, xs, ada, *ffn1, layer=l, sub=0, first=(l == 0), side=side)
        xp, *st_p, xs, sv, ps, ss, ds = _mixer_call(xp, xs, ada, w_in_b, w_out_b, sgu_w, sgu_wsm, sgu_bias,
                                                    small, pool_in, sconv_in, dconv_in, layer=l)
        side = [] if l == DEPTH - 1 else [(ffn1_w_gate, l + 1), (ffn1_w_up, l + 1), (ffn1_w_down, l + 1)]
        xp, xs, ffn1 = ffn(xp, xs, ada, *ffn2, layer=l, sub=2, first=False, side=side)
        outs_p.append(st_p)
        outs_s.append([sv, ps, ss, ds])

    y_prompt = xp.reshape(BATCH, SEQ, D_MODEL)
    y_sample = from_blocks(xs.reshape(n_blk, DEC_SEQ, SAMPLE_BLK, D_MODEL))
    stack_p = lambda i: jnp.stack([o[i] for o in outs_p], axis=0)
    stack_s = lambda i: from_blocks(jnp.stack([o[i] for o in outs_s], axis=0))
    return (y_prompt, y_sample, stack_p(0), stack_s(0), stack_p(1), stack_s(1),
            stack_p(2), stack_s(2), stack_p(3), stack_s(3))
```

```python
import functools

import jax
import jax.numpy as jnp
from jax import lax
from jax.experimental import pallas as pl
from jax.experimental.pallas import tpu as pltpu

D_MODEL = 1024
BATCH = 8
SEQ = 2048
DEPTH = 2
DEC_BATCH = 128
DEC_SEQ = 8
PAST_LEN = 16384
G = 256
HEAD_DIM = 64
A_HEADS = G // HEAD_DIM
CHUNK = 128
POOL_WINDOWS = (2, 4, 8, 16)
POOL_GDIM = G // len(POOL_WINDOWS)
POOL_BUF = 15
SCONV_W = 3
DCONV_W = 31
D_FF = 2816
N_ADA = 9
ALPHA = (2 * DEPTH) ** 0.25
IN_COLS = 8 * G
N_SEQ_ALL = DEC_BATCH + BATCH
ADA_PER_STEP = 3

SUBLANES = 8
BF16_SUBLANES = 16
POOL_CARRY = 16
SCONV_CARRY = 8
DCONV_CARRY = 32
VMEM_LIMIT_BYTES = 56 * 1024 * 1024

TM_FFN = 1024
ROW_PARTS = 2
TM_MIX = 512
SUB_MIX = 256

BF16 = jnp.bfloat16
F32 = jnp.float32


def _ln(x, g, b, eps=1e-5):
    mu = jnp.mean(x, axis=-1, keepdims=True)
    xc = x - mu
    var = jnp.mean(xc * xc, axis=-1, keepdims=True)
    return xc * lax.rsqrt(var + eps) * g + b


def _gelu(x):
    return 0.5 * x * (1.0 + lax.erf(x * 0.7071067811865476))


def _silu(x):
    return x * jax.nn.sigmoid(x)


def _rows(x, m):
    if m.shape[0] == 1 or m.shape[0] == x.shape[0]:
        return m
    reps = x.shape[0] // m.shape[0]
    return jnp.broadcast_to(m[None], (reps,) + m.shape).reshape(x.shape)


def _dot(a, b):
    return jnp.dot(a, b, preferred_element_type=F32)


def _const_spec(shape, index):
    return pl.BlockSpec(shape, lambda *_: index, pipeline_mode=pl.Buffered(1))


def _ada_kernel(c_ref, w_ref, b_ref, o_ref):
    s = _silu(c_ref[...]).astype(BF16)
    y = _dot(s, w_ref[...].astype(BF16))
    for k in range(ADA_PER_STEP):
        cols = slice(k * D_MODEL, (k + 1) * D_MODEL)
        o_ref[k] = y[:, cols] + b_ref[pl.ds(pl.program_id(0), 1), cols]


def _ada_call(c_all, w_ada, b_ada):
    return pl.pallas_call(
        _ada_kernel,
        grid=(DEPTH, N_ADA // ADA_PER_STEP),
        in_specs=[
            pl.BlockSpec((N_SEQ_ALL, D_MODEL), lambda l, k: (0, 0)),
            pl.BlockSpec((None, D_MODEL, ADA_PER_STEP * D_MODEL), lambda l, k: (l, 0, k)),
            pl.BlockSpec((DEPTH, ADA_PER_STEP * D_MODEL), lambda l, k: (0, k)),
        ],
        out_specs=pl.BlockSpec((None, ADA_PER_STEP, N_SEQ_ALL, D_MODEL), lambda l, k: (l, k, 0, 0)),
        out_shape=jax.ShapeDtypeStruct((DEPTH, N_ADA, N_SEQ_ALL, D_MODEL), F32),
        compiler_params=pltpu.CompilerParams(
            dimension_semantics=("arbitrary", "arbitrary"), vmem_limit_bytes=VMEM_LIMIT_BYTES),
        name="ada",
    )(c_all, w_ada, b_ada)


def _ffn_math(x, mod, wg_ref, wu_ref, wd_ref, lng_ref, lnb_ref, *, layer, sub):
    k0 = 0 if sub == 0 else 6
    sh, sc, gt = mod(k0), mod(k0 + 1), mod(k0 + 2)
    xm = (x * (1.0 + _rows(x, sc)) + _rows(x, sh)).astype(BF16)
    hg = _dot(xm, wg_ref[...])
    hu = _dot(xm, wu_ref[...])
    h = (_silu(hg) * hu).astype(BF16)
    gain = 0.5 * (1.0 + _rows(x, gt))
    half = x.shape[0] // 2
    outs = []
    for r in (slice(0, half), slice(half, 2 * half)):
        y = ALPHA * x[r] + (gain if gain.shape[0] == 1 else gain[r]) * _dot(h[r], wd_ref[...])
        outs.append(_ln(y, lng_ref[layer, pl.ds(sub, 1), :], lnb_ref[layer, pl.ds(sub, 1), :]))
    return jnp.concatenate(outs, axis=0)


def _sample_rows(x_ref):
    return jnp.concatenate([x_ref[:, t, :] for t in range(DEC_SEQ)], axis=0)


def _store_sample_rows(o_ref, y):
    for t in range(DEC_SEQ):
        o_ref[:, t, :] = y[t * DEC_BATCH:(t + 1) * DEC_BATCH, :]


def _ffn_kernel(x_ref, ada_ref, wg_ref, wu_ref, wd_ref, lng_ref, lnb_ref, ing_ref, inb_ref, *rest,
                layer, sub, first, prompt, n_side):
    side_in, o_ref, side_out = rest[:n_side], rest[n_side], rest[n_side + 1:]

    def half_step(x, mod):
        if first:
            x = _ln(x, ing_ref[...], inb_ref[...])
        return _ffn_math(x, mod, wg_ref, wu_ref, wd_ref, lng_ref, lnb_ref, layer=layer, sub=sub)

    if prompt:
        b = pl.program_id(0)
        mod = lambda k: ada_ref[k, pl.ds(b, 1), :]
        part = x_ref.shape[0] // ROW_PARTS
        for i in range(ROW_PARTS):
            rows = slice(i * part, (i + 1) * part)
            o_ref[rows, :] = half_step(x_ref[rows, :], mod)
    else:
        _store_sample_rows(o_ref, half_step(_sample_rows(x_ref), lambda k: ada_ref[k]))
    for src, dst in zip(side_in, side_out):
        dst[...] = src[...].astype(BF16)


def _ffn_call(x, ada, wg, wu, wd, post_g, post_b, in_g, in_b, *, layer, sub, first, prompt, side=()):
    if prompt:
        nt = SEQ // TM_FFN
        grid = (BATCH, nt)
        x_spec = pl.BlockSpec((TM_FFN, D_MODEL), lambda b, j: (b * nt + j, 0))
        ada_spec = _const_spec((None, N_ADA, BATCH, D_MODEL), (layer, 0, DEC_BATCH // BATCH, 0))
    else:
        assert not side
        grid = (1, 1)
        x_spec = pl.BlockSpec(x.shape, lambda b, j: (0, 0, 0))
        ada_spec = _const_spec((None, N_ADA, DEC_BATCH, D_MODEL), (layer, 0, 0, 0))
    side_in_specs, side_out_specs, side_shapes = [], [], []
    for w, wl in side:
        k_dim, n_dim = w.shape[1:]
        steps = grid[0] * grid[1]
        while (k_dim // steps) % BF16_SUBLANES:
            steps //= 2
        per = grid[0] * grid[1] // steps
        chunk = lambda b, j, per=per: (b * grid[1] + j) // per
        side_in_specs.append(pl.BlockSpec((None, k_dim // steps, n_dim),
                                          lambda b, j, wl=wl, chunk=chunk: (wl, chunk(b, j), 0)))
        side_out_specs.append(pl.BlockSpec((k_dim // steps, n_dim),
                                           lambda b, j, chunk=chunk: (chunk(b, j), 0)))
        side_shapes.append(jax.ShapeDtypeStruct((k_dim, n_dim), BF16))
    kern = functools.partial(_ffn_kernel, layer=layer, sub=sub, first=first, prompt=prompt,
                             n_side=len(side))
    res = pl.pallas_call(
        kern,
        grid=grid,
        in_specs=[
            x_spec, ada_spec,
            _const_spec((D_MODEL, D_FF), (0, 0)),
            _const_spec((D_MODEL, D_FF), (0, 0)),
            _const_spec((D_FF, D_MODEL), (0, 0)),
            _const_spec((DEPTH, 3, D_MODEL), (0, 0, 0)),
            _const_spec((DEPTH, 3, D_MODEL), (0, 0, 0)),
            _const_spec((1, D_MODEL), (0, 0)),
            _const_spec((1, D_MODEL), (0, 0)),
        ] + side_in_specs,
        out_specs=[x_spec] + side_out_specs,
        out_shape=[jax.ShapeDtypeStruct(x.shape, F32)] + side_shapes,
        compiler_params=pltpu.CompilerParams(
            dimension_semantics=("arbitrary", "arbitrary"), vmem_limit_bytes=VMEM_LIMIT_BYTES),
        name=f"ffn{sub}_{'p' if prompt else 's'}{layer}",
    )(x, ada, wg, wu, wd, post_g, post_b, in_g, in_b, *[w for w, _ in side])
    return res[0], list(res[1:])


def _pool_lane_consts():
    lane = lax.broadcasted_iota(jnp.int32, (1, G), 1)
    grp = lane // POOL_GDIM
    win = jnp.where(grp == 0, float(POOL_WINDOWS[0]),
                    jnp.where(grp == 1, float(POOL_WINDOWS[1]),
                              jnp.where(grp == 2, float(POOL_WINDOWS[2]), float(POOL_WINDOWS[3]))))
    return grp, win


def _pool_select(grp, s2, s4, s8, s16):
    return jnp.where(grp == 0, s2, jnp.where(grp == 1, s4, jnp.where(grp == 2, s8, s16)))


def _rms_group(y, g, eps=1e-6):
    return y * lax.rsqrt(jnp.mean(y * y, axis=-1, keepdims=True) + eps) * g


def _mixer_front(x, mod, win_ref):
    sh, sc = mod(3), mod(4)
    xm = (x * (1.0 + _rows(x, sc)) + _rows(x, sh)).astype(BF16)
    return _dot(xm, win_ref[...])


def _mixer_norm(ya, yb, yc, yd, onorm_ref, layer):
    parts = [ya, yb, yc, yd]
    normed = [_rms_group(p, onorm_ref[pl.ds(layer, 1), i * G:(i + 1) * G]).astype(BF16)
              for i, p in enumerate(parts)]
    return jnp.concatenate(normed, axis=-1)


def _mixer_back(x, mixn, mod, wout_ref, lng_ref, lnb_ref, layer):
    y = ALPHA * x + (1.0 + _rows(x, mod(5))) * _dot(mixn, wout_ref[...])
    return _ln(y, lng_ref[layer, pl.ds(1, 1), :], lnb_ref[layer, pl.ds(1, 1), :])


def _mixer_prompt_mid(proj, pos0, r0, last, sguw_ref, sgub_ref, sgug_ref, sgulb_ref,
                      poolw_ref, pools_ref, sconvw_ref, dconvw_ref, dconvb_ref, cvg_ref, cvb_ref,
                      onorm_ref, sguv_ref, poolst_ref, sconvst_ref, dconvst_ref,
                      pool_ext, sconv_ext, dconv_ext, *, layer):
    n = proj.shape[0]
    lrow = pl.ds(layer, 1)
    col = lambda i: proj[:, i * G:(i + 1) * G]

    u = _gelu(col(0))
    v = _ln(_gelu(col(1)), sgug_ref[lrow, :], sgulb_ref[lrow, :])
    if last:
        sguv_ref[...] = v[n - CHUNK:, :]
    r_i = lax.broadcasted_iota(jnp.int32, (CHUNK, CHUNK), 0)
    c_i = lax.broadcasted_iota(jnp.int32, (CHUNK, CHUNK), 1)
    wcat = jnp.concatenate([jnp.where(r_i >= c_i, sguw_ref[h], 0.0) for h in range(A_HEADS)],
                           axis=1).astype(BF16)
    head = lax.broadcasted_iota(jnp.int32, (1, G), 1) // HEAD_DIM
    bias = sgub_ref[...]
    mixed = []
    for c in range(n // CHUNK):
        vc = v[c * CHUNK:(c + 1) * CHUNK, :]
        rhs = jnp.concatenate([jnp.where(head == h, vc, 0.0) for h in range(A_HEADS)],
                              axis=0).astype(BF16)
        mixed.append(_dot(wcat, rhs) + bias)
    ya = u * jnp.concatenate(mixed, axis=0)

    xb = col(2)
    pool_ext[r0 + POOL_CARRY:r0 + POOL_CARRY + n, :] = xb
    e = pool_ext[r0:r0 + POOL_CARRY + n, :]
    s2 = e + pltpu.roll(e, 1, 0)
    s4 = s2 + pltpu.roll(s2, 2, 0)
    s8 = s4 + pltpu.roll(s4, 4, 0)
    s16 = s8 + pltpu.roll(s8, 8, 0)
    grp, win = _pool_lane_consts()
    ssel = _pool_select(grp, s2, s4, s8, s16)[POOL_CARRY:, :]
    pos = pos0 + lax.broadcasted_iota(jnp.int32, (n, 1), 0)
    cnt = jnp.minimum(win, (pos + 1).astype(F32))
    pooled = (ssel / cnt - xb).astype(BF16)
    yb = _dot(pooled, poolw_ref[...]) * pools_ref[lrow, :]
    if last:
        poolst_ref[...] = pool_ext[r0 + n + POOL_CARRY - POOL_BUF:r0 + n + POOL_CARRY, :]

    zc = col(4) * col(5)
    sconv_ext[r0 + SCONV_CARRY:r0 + SCONV_CARRY + n, :] = zc
    es = sconv_ext[r0:r0 + SCONV_CARRY + n, :]
    yconv = (sconvw_ref[layer, pl.ds(2, 1), :] * es
             + sconvw_ref[layer, pl.ds(1, 1), :] * pltpu.roll(es, 1, 0)
             + sconvw_ref[layer, pl.ds(0, 1), :] * pltpu.roll(es, 2, 0))
    yc = col(3) * yconv[SCONV_CARRY:, :]
    if last:
        sconvst_ref[...] = sconv_ext[r0 + n + SCONV_CARRY - (SCONV_W - 1):r0 + n + SCONV_CARRY, :]

    zd = col(6) * jax.nn.sigmoid(col(7))
    dconv_ext[r0 + DCONV_CARRY:r0 + DCONV_CARRY + n, :] = zd
    if last:
        dconvst_ref[...] = dconv_ext[r0 + n + DCONV_CARRY - (DCONV_W - 1):r0 + n + DCONV_CARRY, :]
    ed = dconv_ext[r0:r0 + DCONV_CARRY + n, :]
    acc = jnp.zeros((n, G), F32)
    for r in range(SUBLANES):
        er = ed if r == 0 else pltpu.roll(ed, r, 0)
        for q in range(DCONV_CARRY // SUBLANES):
            tap = SUBLANES * q + r
            if tap >= DCONV_W:
                continue
            lo = DCONV_CARRY - SUBLANES * q
            acc = acc + dconvw_ref[layer, pl.ds(DCONV_W - 1 - tap, 1), :] * er[lo:lo + n, :]
    hd = _ln(acc + dconvb_ref[lrow, :], cvg_ref[lrow, :], cvb_ref[lrow, :])
    yd = _silu(hd)

    return _mixer_norm(ya, yb, yc, yd, onorm_ref, layer)


def _mixer_prompt_kernel(x_ref, ada_ref, win_ref, wout_ref, sguw_ref, sgub_ref, sgug_ref, sgulb_ref,
                         poolw_ref, pools_ref, sconvw_ref, dconvw_ref, dconvb_ref, cvg_ref, cvb_ref,
                         onorm_ref, lng_ref, lnb_ref, *rest, layer, sub_rows):
    (o_ref, sguv_ref, poolst_ref, sconvst_ref, dconvst_ref,
     pool_ext, sconv_ext, dconv_ext) = rest[-8:]
    tm = x_ref.shape[0]
    b = pl.program_id(0)
    j = pl.program_id(1)

    @pl.when(j == 0)
    def _():
        pool_ext[0:POOL_CARRY, :] = jnp.zeros((POOL_CARRY, G), F32)
        sconv_ext[0:SCONV_CARRY, :] = jnp.zeros((SCONV_CARRY, G), F32)
        dconv_ext[0:DCONV_CARRY, :] = jnp.zeros((DCONV_CARRY, G), F32)

    @pl.when(j > 0)
    def _():
        pool_ext[0:POOL_CARRY, :] = pool_ext[tm:tm + POOL_CARRY, :]
        sconv_ext[0:SCONV_CARRY, :] = sconv_ext[tm:tm + SCONV_CARRY, :]
        dconv_ext[0:DCONV_CARRY, :] = dconv_ext[tm:tm + DCONV_CARRY, :]

    mod = lambda k: ada_ref[k, pl.ds(b, 1), :]
    n_sub = tm // sub_rows
    rows = lambda h: slice(h * sub_rows, (h + 1) * sub_rows)
    front = lambda h: _mixer_front(x_ref[rows(h), :], mod, win_ref)
    proj = front(0)
    for h in range(n_sub):
        proj_next = front(h + 1) if h + 1 < n_sub else None
        mixn = _mixer_prompt_mid(
            proj, j * tm + h * sub_rows, h * sub_rows, h == n_sub - 1,
            sguw_ref, sgub_ref, sgug_ref, sgulb_ref, poolw_ref, pools_ref, sconvw_ref,
            dconvw_ref, dconvb_ref, cvg_ref, cvb_ref, onorm_ref,
            sguv_ref, poolst_ref, sconvst_ref, dconvst_ref, pool_ext, sconv_ext, dconv_ext, layer=layer)
        o_ref[rows(h), :] = _mixer_back(x_ref[rows(h), :], mixn, mod, wout_ref, lng_ref, lnb_ref, layer)
        proj = proj_next


def _stacked_state_specs(prev, shapes, out_block):
    n_in_alias = len(prev)
    alias_specs = [pl.BlockSpec(memory_space=pl.ANY)] * n_in_alias
    out_shapes = [jax.ShapeDtypeStruct((DEPTH,) + shp, F32) for shp in shapes]
    out_specs = [out_block(shp) for shp in shapes]
    return alias_specs, out_shapes, out_specs, n_in_alias


def _mixer_prompt_call(x2d, ada, w_in, w_out, sgu_w, sgu_bias, small, prev_states, *, layer):
    (sgu_ln_g, sgu_ln_b, pool_bd, pool_scale, sconv_w, dconv_w, dconv_b, conv_ln_g, conv_ln_b,
     out_norm_g, post_g, post_b) = small
    tm = TM_MIX
    nt = SEQ // tm
    x_spec = pl.BlockSpec((tm, D_MODEL), lambda b, j: (b * nt + j, 0))
    full2 = lambda a: _const_spec(a.shape, (0,) * a.ndim)
    state_shapes = [(BATCH, n, G) for n in (CHUNK, POOL_BUF, SCONV_W - 1, DCONV_W - 1)]
    st_block = lambda shp: pl.BlockSpec((None, None) + shp[1:], lambda b, j: (layer, b, 0, 0))
    alias_specs, st_shapes, st_specs, n_alias = _stacked_state_specs(prev_states, state_shapes, st_block)
    n_in = 18
    return pl.pallas_call(
        functools.partial(_mixer_prompt_kernel, layer=layer, sub_rows=SUB_MIX),
        grid=(BATCH, nt),
        in_specs=[
            x_spec,
            _const_spec((None, N_ADA, BATCH, D_MODEL), (layer, 0, DEC_BATCH // BATCH, 0)),
            _const_spec((D_MODEL, IN_COLS), (0, 0)),
            _const_spec((D_MODEL, D_MODEL), (0, 0)),
            _const_spec((None, A_HEADS, CHUNK, CHUNK), (layer, 0, 0, 0)),
            _const_spec((None, CHUNK, G), (layer, 0, 0)),
            full2(sgu_ln_g), full2(sgu_ln_b),
            _const_spec((None, G, G), (layer, 0, 0)),
            full2(pool_scale), full2(sconv_w), full2(dconv_w), full2(dconv_b),
            full2(conv_ln_g), full2(conv_ln_b), full2(out_norm_g), full2(post_g), full2(post_b),
        ] + alias_specs,
        out_specs=[x_spec] + st_specs,
        out_shape=[jax.ShapeDtypeStruct((BATCH * SEQ, D_MODEL), F32)] + st_shapes,
        input_output_aliases={n_in + i: 1 + i for i in range(n_alias)},
        scratch_shapes=[
            pltpu.VMEM((tm + POOL_CARRY, G), F32),
            pltpu.VMEM((tm + SCONV_CARRY, G), F32),
            pltpu.VMEM((tm + DCONV_CARRY, G), F32),
        ],
        compiler_params=pltpu.CompilerParams(
            dimension_semantics=("arbitrary", "arbitrary"), vmem_limit_bytes=VMEM_LIMIT_BYTES),
        name=f"mixer_p{layer}",
    )(x2d, ada, w_in, w_out, sgu_w, sgu_bias, sgu_ln_g, sgu_ln_b, pool_bd, pool_scale, sconv_w,
      dconv_w, dconv_b, conv_ln_g, conv_ln_b, out_norm_g, post_g, post_b, *prev_states)


def _mixer_sample_kernel(x_ref, ada_ref, win_ref, wout_ref, sguw_ref, sgub_ref, sgug_ref, sgulb_ref,
                         poolw_ref, pools_ref, sconvw_ref, dconvw_ref, dconvb_ref, cvg_ref, cvb_ref,
                         onorm_ref, lng_ref, lnb_ref, poolin_ref, sconvin_ref, dconvin_ref, *rest, layer):
    o_ref, sguv_ref, poolst_ref, sconvst_ref, dconvst_ref = rest[-5:]
    nb = DEC_BATCH
    lrow = pl.ds(layer, 1)
    x = _sample_rows(x_ref)
    mod = lambda k: ada_ref[k]
    proj = _mixer_front(x, mod, win_ref)
    col = lambda i: proj[:, i * G:(i + 1) * G]
    slab = lambda a, t: a[t * nb:(t + 1) * nb, :]

    u = _gelu(col(0))
    v = _ln(_gelu(col(1)), sgug_ref[lrow, :], sgulb_ref[lrow, :])
    for t in range(DEC_SEQ):
        sguv_ref[:, t, :] = slab(v, t)
    mixed = []
    for t in range(DEC_SEQ):
        m = jnp.zeros((nb, G), F32) + sgub_ref[pl.ds(t, 1), :]
        for s in range(t + 1):
            m = m + sguw_ref[t, pl.ds(s, 1), :] * slab(v, s)
        mixed.append(m)
    ya = u * jnp.concatenate(mixed, axis=0)

    xb = col(2)
    e = [poolin_ref[:, i, :] for i in range(POOL_BUF)] + [slab(xb, t) for t in range(DEC_SEQ)]
    n_e = len(e)
    s2 = [e[i] + e[i - 1] if i >= 1 else None for i in range(n_e)]
    s4 = [s2[i] + s2[i - 2] if i >= 3 else None for i in range(n_e)]
    s8 = [s4[i] + s4[i - 4] if i >= 7 else None for i in range(n_e)]
    s16 = [s8[i] + s8[i - 8] if i >= 15 else None for i in range(n_e)]
    grp, win = _pool_lane_consts()
    cnt = jnp.minimum(win, float(PAST_LEN + 1))
    sel = jnp.concatenate([_pool_select(grp, s2[i], s4[i], s8[i], s16[i])
                           for i in range(POOL_BUF, n_e)], axis=0)
    pooled = (sel / cnt - xb).astype(BF16)
    yb = _dot(pooled, poolw_ref[...]) * pools_ref[lrow, :]
    for i in range(POOL_BUF):
        poolst_ref[:, i, :] = e[n_e - POOL_BUF + i]

    zc = col(4) * col(5)
    es = [sconvin_ref[:, i, :] for i in range(SCONV_W - 1)] + [slab(zc, t) for t in range(DEC_SEQ)]
    yconv = jnp.concatenate(
        [sum(sconvw_ref[layer, pl.ds(k, 1), :] * es[t + k] for k in range(SCONV_W))
         for t in range(DEC_SEQ)], axis=0)
    yc = col(3) * yconv
    for i in range(SCONV_W - 1):
        sconvst_ref[:, i, :] = es[len(es) - (SCONV_W - 1) + i]

    zd = col(6) * jax.nn.sigmoid(col(7))
    ed = [dconvin_ref[:, i, :] for i in range(DCONV_W - 1)] + [slab(zd, t) for t in range(DEC_SEQ)]
    for i in range(DCONV_W - 1):
        dconvst_ref[:, i, :] = ed[DEC_SEQ + i]
    conv = []
    for t in range(DEC_SEQ):
        a = jnp.zeros((nb, G), F32)
        for k in range(DCONV_W):
            a = a + dconvw_ref[layer, pl.ds(k, 1), :] * ed[t + k]
        conv.append(a)
    hd = _ln(jnp.concatenate(conv, axis=0) + dconvb_ref[lrow, :], cvg_ref[lrow, :], cvb_ref[lrow, :])
    yd = _silu(hd)

    mixn = _mixer_norm(ya, yb, yc, yd, onorm_ref, layer)
    _store_sample_rows(o_ref, _mixer_back(x, mixn, mod, wout_ref, lng_ref, lnb_ref, layer))


def _mixer_sample_call(x, ada, w_in, w_out, sgu_wsm, sgu_bias, small, pool_in, sconv_in, dconv_in,
                       prev_states, *, layer):
    (sgu_ln_g, sgu_ln_b, pool_bd, pool_scale, sconv_w, dconv_w, dconv_b, conv_ln_g, conv_ln_b,
     out_norm_g, post_g, post_b) = small
    full2 = lambda a: _const_spec(a.shape, (0,) * a.ndim)
    st_in = lambda a: _const_spec((None,) + a.shape[1:], (layer, 0, 0, 0))
    state_shapes = [(DEC_BATCH, n, G) for n in (DEC_SEQ, POOL_BUF, SCONV_W - 1, DCONV_W - 1)]
    st_block = lambda shp: pl.BlockSpec((None,) + shp, lambda i: (layer, 0, 0, 0))
    alias_specs, st_shapes, st_specs, n_alias = _stacked_state_specs(prev_states, state_shapes, st_block)
    n_in = 21
    return pl.pallas_call(
        functools.partial(_mixer_sample_kernel, layer=layer),
        grid=(1,),
        in_specs=[
            _const_spec(x.shape, (0, 0, 0)),
            _const_spec((None, N_ADA, DEC_BATCH, D_MODEL), (layer, 0, 0, 0)),
            _const_spec((D_MODEL, IN_COLS), (0, 0)),
            _const_spec((D_MODEL, D_MODEL), (0, 0)),
            _const_spec((None, DEC_SEQ, DEC_SEQ, G), (layer, 0, 0, 0)),
            _const_spec((None, DEC_SEQ, G), (layer, 0, 0)),
            full2(sgu_ln_g), full2(sgu_ln_b),
            _const_spec((None, G, G), (layer, 0, 0)),
            full2(pool_scale), full2(sconv_w), full2(dconv_w), full2(dconv_b),
            full2(conv_ln_g), full2(conv_ln_b), full2(out_norm_g), full2(post_g), full2(post_b),
            st_in(pool_in), st_in(sconv_in), st_in(dconv_in),
        ] + alias_specs,
        out_specs=[pl.BlockSpec(x.shape, lambda i: (0, 0, 0))] + st_specs,
        out_shape=[jax.ShapeDtypeStruct(x.shape, F32)] + st_shapes,
        input_output_aliases={n_in + i: 1 + i for i in range(n_alias)},
        compiler_params=pltpu.CompilerParams(
            dimension_semantics=("arbitrary",), vmem_limit_bytes=VMEM_LIMIT_BYTES),
        name=f"mixer_s{layer}",
    )(x, ada, w_in, w_out, sgu_wsm, sgu_bias, sgu_ln_g, sgu_ln_b, pool_bd, pool_scale, sconv_w,
      dconv_w, dconv_b, conv_ln_g, conv_ln_b, out_norm_g, post_g, post_b, pool_in, sconv_in, dconv_in,
      *prev_states)


def kernel(x_prompt, x_sample, state_pool, state_sconv, state_dconv, c_prompt, c_sample, ln_in_g, ln_in_b, w_ada, b_ada, ffn1_w_gate, ffn1_w_up, ffn1_w_down, w_in, sgu_ln_g, sgu_ln_b, sgu_w, sgu_b, pool_w, pool_scale, sconv_w, dconv_w, dconv_b, conv_ln_g, conv_ln_b, out_norm_g, w_out, ffn2_w_gate, ffn2_w_up, ffn2_w_down, post_ln_g, post_ln_b):
    eye = jnp.eye(len(POOL_WINDOWS), dtype=F32)
    pool_bd = jnp.einsum('lgcd,gh->lgchd', pool_w, eye).reshape(DEPTH, G, G).astype(BF16)
    sgu_bias = jnp.repeat(jnp.transpose(sgu_b, (0, 2, 1)), HEAD_DIM, axis=-1)
    sgu_wsm = jnp.repeat(jnp.transpose(sgu_w[:, :, :DEC_SEQ, :DEC_SEQ], (0, 2, 3, 1)),
                         HEAD_DIM, axis=-1)
    in_g, in_b = ln_in_g.reshape(1, D_MODEL), ln_in_b.reshape(1, D_MODEL)
    small = (sgu_ln_g, sgu_ln_b, pool_bd, pool_scale, sconv_w, dconv_w, dconv_b, conv_ln_g,
             conv_ln_b, out_norm_g, post_ln_g, post_ln_b)

    ada = _ada_call(jnp.concatenate([c_sample, c_prompt], axis=0), w_ada, b_ada)

    xs = x_sample
    xp = x_prompt.reshape(BATCH * SEQ, D_MODEL)

    ffn1 = [w[0].astype(BF16) for w in (ffn1_w_gate, ffn1_w_up, ffn1_w_down)]
    ffn = functools.partial(_ffn_call, post_g=post_ln_g, post_b=post_ln_b, in_g=in_g, in_b=in_b)
    st_p, st_s = [], []
    for l in range(DEPTH):
        last = l == DEPTH - 1
        side = [(w_in, l), (w_out, l), (ffn2_w_gate, l), (ffn2_w_up, l), (ffn2_w_down, l)]
        xp, (w_in_b, w_out_b, *ffn2) = ffn(xp, ada, *ffn1, layer=l, sub=0, first=(l == 0), prompt=True,
                                           side=side)
        xs, _ = ffn(xs, ada, *ffn1, layer=l, sub=0, first=(l == 0), prompt=False)
        xp, *st_p = _mixer_prompt_call(xp, ada, w_in_b, w_out_b, sgu_w, sgu_bias, small, st_p, layer=l)
        xs, *st_s = _mixer_sample_call(xs, ada, w_in_b, w_out_b, sgu_wsm, sgu_bias, small,
                                       state_pool, state_sconv, state_dconv, st_s, layer=l)
        side = [] if last else [(ffn1_w_gate, l + 1), (ffn1_w_up, l + 1), (ffn1_w_down, l + 1)]
        xp, ffn1_next = ffn(xp, ada, *ffn2, layer=l, sub=2, first=False, prompt=True, side=side)
        xs, _ = ffn(xs, ada, *ffn2, layer=l, sub=2, first=False, prompt=False)
        ffn1 = ffn1_next

    y_prompt = xp.reshape(BATCH, SEQ, D_MODEL)
    return (y_prompt, xs, st_p[0], st_s[0], st_p[1], st_s[1], st_p[2], st_s[2], st_p[3], st_s[3])
```

```python
import functools

import jax
import jax.numpy as jnp
from jax import lax
from jax.experimental import pallas as pl
from jax.experimental.pallas import tpu as pltpu

D_MODEL = 1024
BATCH = 8
SEQ = 2048
DEPTH = 2
DEC_BATCH = 128
DEC_SEQ = 8
PAST_LEN = 16384
G = 256
HEAD_DIM = 64
A_HEADS = G // HEAD_DIM
CHUNK = 128
POOL_WINDOWS = (2, 4, 8, 16)
POOL_GDIM = G // len(POOL_WINDOWS)
POOL_BUF = 15
SCONV_W = 3
DCONV_W = 31
D_FF = 2816
N_ADA = 9
ALPHA = (2 * DEPTH) ** 0.25
IN_COLS = 8 * G
N_SEQ_ALL = DEC_BATCH + BATCH
ADA_PER_STEP = 3

SUBLANES = 8
BF16_SUBLANES = 16
POOL_CARRY = 16
SCONV_CARRY = 8
DCONV_CARRY = 32
VMEM_LIMIT_BYTES = 56 * 1024 * 1024

TM_FFN = 1024
ROW_PARTS = 2
TM_MIX = 1024
SUB_MIX = 512

BF16 = jnp.bfloat16
F32 = jnp.float32


def _ln(x, g, b, eps=1e-5):
    mu = jnp.mean(x, axis=-1, keepdims=True)
    xc = x - mu
    var = jnp.mean(xc * xc, axis=-1, keepdims=True)
    return xc * lax.rsqrt(var + eps) * g + b


def _gelu(x):
    return 0.5 * x * (1.0 + lax.erf(x * 0.7071067811865476))


def _silu(x):
    return x * jax.nn.sigmoid(x)


def _rows(x, m):
    if m.shape[0] == 1 or m.shape[0] == x.shape[0]:
        return m
    reps = x.shape[0] // m.shape[0]
    return jnp.broadcast_to(m[None], (reps,) + m.shape).reshape(x.shape)


def _dot(a, b):
    return jnp.dot(a, b, preferred_element_type=F32)


def _const_spec(shape, index):
    return pl.BlockSpec(shape, lambda *_: index, pipeline_mode=pl.Buffered(1))


def _ada_kernel(c_ref, w_ref, b_ref, o_ref):
    s = _silu(c_ref[...]).astype(BF16)
    y = _dot(s, w_ref[...].astype(BF16))
    for k in range(ADA_PER_STEP):
        cols = slice(k * D_MODEL, (k + 1) * D_MODEL)
        o_ref[k] = y[:, cols] + b_ref[pl.ds(pl.program_id(0), 1), cols]


def _ada_call(c_all, w_ada, b_ada):
    return pl.pallas_call(
        _ada_kernel,
        grid=(DEPTH, N_ADA // ADA_PER_STEP),
        in_specs=[
            pl.BlockSpec((N_SEQ_ALL, D_MODEL), lambda l, k: (0, 0)),
            pl.BlockSpec((None, D_MODEL, ADA_PER_STEP * D_MODEL), lambda l, k: (l, 0, k)),
            pl.BlockSpec((DEPTH, ADA_PER_STEP * D_MODEL), lambda l, k: (0, k)),
        ],
        out_specs=pl.BlockSpec((None, ADA_PER_STEP, N_SEQ_ALL, D_MODEL), lambda l, k: (l, k, 0, 0)),
        out_shape=jax.ShapeDtypeStruct((DEPTH, N_ADA, N_SEQ_ALL, D_MODEL), F32),
        compiler_params=pltpu.CompilerParams(
            dimension_semantics=("arbitrary", "arbitrary"), vmem_limit_bytes=VMEM_LIMIT_BYTES),
        name="ada",
    )(c_all, w_ada, b_ada)


def _ffn_math(x, mod, wg_ref, wu_ref, wd_ref, lng_ref, lnb_ref, *, layer, sub):
    k0 = 0 if sub == 0 else 6
    sh, sc, gt = mod(k0), mod(k0 + 1), mod(k0 + 2)
    xm = (x * (1.0 + _rows(x, sc)) + _rows(x, sh)).astype(BF16)
    hg = _dot(xm, wg_ref[...])
    hu = _dot(xm, wu_ref[...])
    h = (_silu(hg) * hu).astype(BF16)
    gain = 0.5 * (1.0 + _rows(x, gt))
    half = x.shape[0] // 2
    outs = []
    for r in (slice(0, half), slice(half, 2 * half)):
        y = ALPHA * x[r] + (gain if gain.shape[0] == 1 else gain[r]) * _dot(h[r], wd_ref[...])
        outs.append(_ln(y, lng_ref[layer, pl.ds(sub, 1), :], lnb_ref[layer, pl.ds(sub, 1), :]))
    return jnp.concatenate(outs, axis=0)


def _ffn_kernel(x_ref, ada_ref, wg_ref, wu_ref, wd_ref, lng_ref, lnb_ref, ing_ref, inb_ref, *rest,
                layer, sub, first, prompt, n_side):
    side_in, o_ref, side_out = rest[:n_side], rest[n_side], rest[n_side + 1:]
    if prompt:
        b = pl.program_id(0)
        mod = lambda k: ada_ref[k, pl.ds(b, 1), :]
    else:
        mod = lambda k: ada_ref[k]
    n_part = ROW_PARTS if prompt else 1
    part = x_ref.shape[0] // n_part
    for i in range(n_part):
        rows = slice(i * part, (i + 1) * part)
        x = x_ref[rows, :]
        if first:
            x = _ln(x, ing_ref[...], inb_ref[...])
        o_ref[rows, :] = _ffn_math(x, mod, wg_ref, wu_ref, wd_ref, lng_ref, lnb_ref, layer=layer, sub=sub)
    for src, dst in zip(side_in, side_out):
        dst[...] = src[...].astype(BF16)


def _ffn_call(x2d, ada, wg, wu, wd, post_g, post_b, in_g, in_b, *, layer, sub, first, prompt, side=()):
    rows = x2d.shape[0]
    if prompt:
        nt = SEQ // TM_FFN
        grid = (BATCH, nt)
        x_spec = pl.BlockSpec((TM_FFN, D_MODEL), lambda b, j: (b * nt + j, 0))
        ada_spec = _const_spec((None, N_ADA, BATCH, D_MODEL), (layer, 0, DEC_BATCH // BATCH, 0))
    else:
        assert not side
        grid = (1, 1)
        x_spec = pl.BlockSpec((rows, D_MODEL), lambda b, j: (0, 0))
        ada_spec = _const_spec((None, N_ADA, DEC_BATCH, D_MODEL), (layer, 0, 0, 0))
    side_in_specs, side_out_specs, side_shapes = [], [], []
    for w, wl in side:
        k_dim, n_dim = w.shape[1:]
        steps = grid[0] * grid[1]
        while (k_dim // steps) % BF16_SUBLANES:
            steps //= 2
        per = grid[0] * grid[1] // steps
        chunk = lambda b, j, per=per: (b * grid[1] + j) // per
        side_in_specs.append(pl.BlockSpec((None, k_dim // steps, n_dim),
                                          lambda b, j, wl=wl, chunk=chunk: (wl, chunk(b, j), 0)))
        side_out_specs.append(pl.BlockSpec((k_dim // steps, n_dim),
                                           lambda b, j, chunk=chunk: (chunk(b, j), 0)))
        side_shapes.append(jax.ShapeDtypeStruct((k_dim, n_dim), BF16))
    kern = functools.partial(_ffn_kernel, layer=layer, sub=sub, first=first, prompt=prompt,
                             n_side=len(side))
    res = pl.pallas_call(
        kern,
        grid=grid,
        in_specs=[
            x_spec, ada_spec,
            _const_spec((D_MODEL, D_FF), (0, 0)),
            _const_spec((D_MODEL, D_FF), (0, 0)),
            _const_spec((D_FF, D_MODEL), (0, 0)),
            _const_spec((DEPTH, 3, D_MODEL), (0, 0, 0)),
            _const_spec((DEPTH, 3, D_MODEL), (0, 0, 0)),
            _const_spec((1, D_MODEL), (0, 0)),
            _const_spec((1, D_MODEL), (0, 0)),
        ] + side_in_specs,
        out_specs=[x_spec] + side_out_specs,
        out_shape=[jax.ShapeDtypeStruct((rows, D_MODEL), F32)] + side_shapes,
        compiler_params=pltpu.CompilerParams(
            dimension_semantics=("arbitrary", "arbitrary"), vmem_limit_bytes=VMEM_LIMIT_BYTES),
        name=f"ffn{sub}_{'p' if prompt else 's'}{layer}",
    )(x2d, ada, wg, wu, wd, post_g, post_b, in_g, in_b, *[w for w, _ in side])
    return res[0], list(res[1:])


def _pool_lane_consts():
    lane = lax.broadcasted_iota(jnp.int32, (1, G), 1)
    grp = lane // POOL_GDIM
    win = jnp.where(grp == 0, float(POOL_WINDOWS[0]),
                    jnp.where(grp == 1, float(POOL_WINDOWS[1]),
                              jnp.where(grp == 2, float(POOL_WINDOWS[2]), float(POOL_WINDOWS[3]))))
    return grp, win


def _pool_select(grp, s2, s4, s8, s16):
    return jnp.where(grp == 0, s2, jnp.where(grp == 1, s4, jnp.where(grp == 2, s8, s16)))


def _rms_group(y, g, eps=1e-6):
    return y * lax.rsqrt(jnp.mean(y * y, axis=-1, keepdims=True) + eps) * g


def _mixer_front(x, mod, win_ref):
    sh, sc = mod(3), mod(4)
    xm = (x * (1.0 + _rows(x, sc)) + _rows(x, sh)).astype(BF16)
    return _dot(xm, win_ref[...])


def _mixer_norm(ya, yb, yc, yd, onorm_ref, layer):
    parts = [ya, yb, yc, yd]
    normed = [_rms_group(p, onorm_ref[pl.ds(layer, 1), i * G:(i + 1) * G]).astype(BF16)
              for i, p in enumerate(parts)]
    return jnp.concatenate(normed, axis=-1)


def _mixer_back(x, mixn, mod, wout_ref, lng_ref, lnb_ref, layer):
    y = ALPHA * x + (1.0 + _rows(x, mod(5))) * _dot(mixn, wout_ref[...])
    return _ln(y, lng_ref[layer, pl.ds(1, 1), :], lnb_ref[layer, pl.ds(1, 1), :])


def _mixer_prompt_mid(proj, pos0, r0, last, sguw_ref, sgub_ref, sgug_ref, sgulb_ref,
                      poolw_ref, pools_ref, sconvw_ref, dconvw_ref, dconvb_ref, cvg_ref, cvb_ref,
                      onorm_ref, sguv_ref, poolst_ref, sconvst_ref, dconvst_ref,
                      pool_ext, sconv_ext, dconv_ext, *, layer):
    n = proj.shape[0]
    lrow = pl.ds(layer, 1)
    col = lambda i: proj[:, i * G:(i + 1) * G]

    u = _gelu(col(0))
    v = _ln(_gelu(col(1)), sgug_ref[lrow, :], sgulb_ref[lrow, :])
    if last:
        sguv_ref[...] = v[n - CHUNK:, :]
    r_i = lax.broadcasted_iota(jnp.int32, (CHUNK, CHUNK), 0)
    c_i = lax.broadcasted_iota(jnp.int32, (CHUNK, CHUNK), 1)
    wcat = jnp.concatenate([jnp.where(r_i >= c_i, sguw_ref[h], 0.0) for h in range(A_HEADS)],
                           axis=1).astype(BF16)
    head = lax.broadcasted_iota(jnp.int32, (1, G), 1) // HEAD_DIM
    bias = sgub_ref[...]
    mixed = []
    for c in range(n // CHUNK):
        vc = v[c * CHUNK:(c + 1) * CHUNK, :]
        rhs = jnp.concatenate([jnp.where(head == h, vc, 0.0) for h in range(A_HEADS)],
                              axis=0).astype(BF16)
        mixed.append(_dot(wcat, rhs) + bias)
    ya = u * jnp.concatenate(mixed, axis=0)

    xb = col(2)
    pool_ext[r0 + POOL_CARRY:r0 + POOL_CARRY + n, :] = xb
    e = pool_ext[r0:r0 + POOL_CARRY + n, :]
    s2 = e + pltpu.roll(e, 1, 0)
    s4 = s2 + pltpu.roll(s2, 2, 0)
    s8 = s4 + pltpu.roll(s4, 4, 0)
    s16 = s8 + pltpu.roll(s8, 8, 0)
    grp, win = _pool_lane_consts()
    ssel = _pool_select(grp, s2, s4, s8, s16)[POOL_CARRY:, :]
    pos = pos0 + lax.broadcasted_iota(jnp.int32, (n, 1), 0)
    cnt = jnp.minimum(win, (pos + 1).astype(F32))
    pooled = (ssel / cnt - xb).astype(BF16)
    yb = _dot(pooled, poolw_ref[...]) * pools_ref[lrow, :]
    if last:
        poolst_ref[...] = pool_ext[r0 + n + POOL_CARRY - POOL_BUF:r0 + n + POOL_CARRY, :]

    zc = col(4) * col(5)
    sconv_ext[r0 + SCONV_CARRY:r0 + SCONV_CARRY + n, :] = zc
    es = sconv_ext[r0:r0 + SCONV_CARRY + n, :]
    yconv = (sconvw_ref[layer, pl.ds(2, 1), :] * es
             + sconvw_ref[layer, pl.ds(1, 1), :] * pltpu.roll(es, 1, 0)
             + sconvw_ref[layer, pl.ds(0, 1), :] * pltpu.roll(es, 2, 0))
    yc = col(3) * yconv[SCONV_CARRY:, :]
    if last:
        sconvst_ref[...] = sconv_ext[r0 + n + SCONV_CARRY - (SCONV_W - 1):r0 + n + SCONV_CARRY, :]

    zd = col(6) * jax.nn.sigmoid(col(7))
    dconv_ext[r0 + DCONV_CARRY:r0 + DCONV_CARRY + n, :] = zd
    if last:
        dconvst_ref[...] = dconv_ext[r0 + n + DCONV_CARRY - (DCONV_W - 1):r0 + n + DCONV_CARRY, :]
    ed = dconv_ext[r0:r0 + DCONV_CARRY + n, :]
    acc = jnp.zeros((n, G), F32)
    for r in range(SUBLANES):
        er = ed if r == 0 else pltpu.roll(ed, r, 0)
        for q in range(DCONV_CARRY // SUBLANES):
            tap = SUBLANES * q + r
            if tap >= DCONV_W:
                continue
            lo = DCONV_CARRY - SUBLANES * q
            acc = acc + dconvw_ref[layer, pl.ds(DCONV_W - 1 - tap, 1), :] * er[lo:lo + n, :]
    hd = _ln(acc + dconvb_ref[lrow, :], cvg_ref[lrow, :], cvb_ref[lrow, :])
    yd = _silu(hd)

    return _mixer_norm(ya, yb, yc, yd, onorm_ref, layer)


def _mixer_prompt_kernel(x_ref, ada_ref, win_ref, wout_ref, sguw_ref, sgub_ref, sgug_ref, sgulb_ref,
                         poolw_ref, pools_ref, sconvw_ref, dconvw_ref, dconvb_ref, cvg_ref, cvb_ref,
                         onorm_ref, lng_ref, lnb_ref,
                         o_ref, sguv_ref, poolst_ref, sconvst_ref, dconvst_ref,
                         pool_ext, sconv_ext, dconv_ext, *, layer, sub_rows):
    tm = x_ref.shape[0]
    b = pl.program_id(0)
    j = pl.program_id(1)

    @pl.when(j == 0)
    def _():
        pool_ext[0:POOL_CARRY, :] = jnp.zeros((POOL_CARRY, G), F32)
        sconv_ext[0:SCONV_CARRY, :] = jnp.zeros((SCONV_CARRY, G), F32)
        dconv_ext[0:DCONV_CARRY, :] = jnp.zeros((DCONV_CARRY, G), F32)

    @pl.when(j > 0)
    def _():
        pool_ext[0:POOL_CARRY, :] = pool_ext[tm:tm + POOL_CARRY, :]
        sconv_ext[0:SCONV_CARRY, :] = sconv_ext[tm:tm + SCONV_CARRY, :]
        dconv_ext[0:DCONV_CARRY, :] = dconv_ext[tm:tm + DCONV_CARRY, :]

    mod = lambda k: ada_ref[k, pl.ds(b, 1), :]
    n_sub = tm // sub_rows
    rows = lambda h: slice(h * sub_rows, (h + 1) * sub_rows)
    front = lambda h: _mixer_front(x_ref[rows(h), :], mod, win_ref)
    proj = front(0)
    for h in range(n_sub):
        proj_next = front(h + 1) if h + 1 < n_sub else None
        mixn = _mixer_prompt_mid(
            proj, j * tm + h * sub_rows, h * sub_rows, h == n_sub - 1,
            sguw_ref, sgub_ref, sgug_ref, sgulb_ref, poolw_ref, pools_ref, sconvw_ref,
            dconvw_ref, dconvb_ref, cvg_ref, cvb_ref, onorm_ref,
            sguv_ref, poolst_ref, sconvst_ref, dconvst_ref, pool_ext, sconv_ext, dconv_ext, layer=layer)
        o_ref[rows(h), :] = _mixer_back(x_ref[rows(h), :], mixn, mod, wout_ref, lng_ref, lnb_ref, layer)
        proj = proj_next


def _mixer_prompt_call(x2d, ada, w_in, w_out, sgu_w, sgu_bias, small, *, layer):
    (sgu_ln_g, sgu_ln_b, pool_bd, pool_scale, sconv_w, dconv_w, dconv_b, conv_ln_g, conv_ln_b,
     out_norm_g, post_g, post_b) = small
    tm = TM_MIX
    nt = SEQ // tm
    x_spec = pl.BlockSpec((tm, D_MODEL), lambda b, j: (b * nt + j, 0))
    st_spec = lambda n: pl.BlockSpec((None, n, G), lambda b, j: (b, 0, 0))
    full2 = lambda a: _const_spec(a.shape, (0,) * a.ndim)
    return pl.pallas_call(
        functools.partial(_mixer_prompt_kernel, layer=layer, sub_rows=SUB_MIX),
        grid=(BATCH, nt),
        in_specs=[
            x_spec,
            _const_spec((None, N_ADA, BATCH, D_MODEL), (layer, 0, DEC_BATCH // BATCH, 0)),
            _const_spec((D_MODEL, IN_COLS), (0, 0)),
            _const_spec((D_MODEL, D_MODEL), (0, 0)),
            _const_spec((None, A_HEADS, CHUNK, CHUNK), (layer, 0, 0, 0)),
            _const_spec((None, CHUNK, G), (layer, 0, 0)),
            full2(sgu_ln_g), full2(sgu_ln_b),
            _const_spec((None, G, G), (layer, 0, 0)),
            full2(pool_scale), full2(sconv_w), full2(dconv_w), full2(dconv_b),
            full2(conv_ln_g), full2(conv_ln_b), full2(out_norm_g), full2(post_g), full2(post_b),
        ],
        out_specs=[x_spec, st_spec(CHUNK), st_spec(POOL_BUF), st_spec(SCONV_W - 1), st_spec(DCONV_W - 1)],
        out_shape=[
            jax.ShapeDtypeStruct((BATCH * SEQ, D_MODEL), F32),
            jax.ShapeDtypeStruct((BATCH, CHUNK, G), F32),
            jax.ShapeDtypeStruct((BATCH, POOL_BUF, G), F32),
            jax.ShapeDtypeStruct((BATCH, SCONV_W - 1, G), F32),
            jax.ShapeDtypeStruct((BATCH, DCONV_W - 1, G), F32),
        ],
        scratch_shapes=[
            pltpu.VMEM((tm + POOL_CARRY, G), F32),
            pltpu.VMEM((tm + SCONV_CARRY, G), F32),
            pltpu.VMEM((tm + DCONV_CARRY, G), F32),
        ],
        compiler_params=pltpu.CompilerParams(
            dimension_semantics=("arbitrary", "arbitrary"), vmem_limit_bytes=VMEM_LIMIT_BYTES),
        name=f"mixer_p{layer}",
    )(x2d, ada, w_in, w_out, sgu_w, sgu_bias, sgu_ln_g, sgu_ln_b, pool_bd, pool_scale, sconv_w,
      dconv_w, dconv_b, conv_ln_g, conv_ln_b, out_norm_g, post_g, post_b)


def _mixer_sample_kernel(x_ref, ada_ref, win_ref, wout_ref, sguw_ref, sgub_ref, sgug_ref, sgulb_ref,
                         poolw_ref, pools_ref, sconvw_ref, dconvw_ref, dconvb_ref, cvg_ref, cvb_ref,
                         onorm_ref, lng_ref, lnb_ref, poolin_ref, sconvin_ref, dconvin_ref,
                         o_ref, sguv_ref, poolst_ref, sconvst_ref, dconvst_ref, *, layer):
    nb = DEC_BATCH
    lrow = pl.ds(layer, 1)
    x = x_ref[...]
    mod = lambda k: ada_ref[k]
    proj = _mixer_front(x, mod, win_ref)
    col = lambda i: proj[:, i * G:(i + 1) * G]
    slab = lambda a, t: a[t * nb:(t + 1) * nb, :]

    u = _gelu(col(0))
    v = _ln(_gelu(col(1)), sgug_ref[lrow, :], sgulb_ref[lrow, :])
    for t in range(DEC_SEQ):
        sguv_ref[t] = slab(v, t)
    mixed = []
    for t in range(DEC_SEQ):
        m = jnp.zeros((nb, G), F32) + sgub_ref[pl.ds(t, 1), :]
        for s in range(t + 1):
            m = m + sguw_ref[t, pl.ds(s, 1), :] * slab(v, s)
        mixed.append(m)
    ya = u * jnp.concatenate(mixed, axis=0)

    xb = col(2)
    e = [poolin_ref[i] for i in range(POOL_BUF)] + [slab(xb, t) for t in range(DEC_SEQ)]
    n_e = len(e)
    s2 = [e[i] + e[i - 1] if i >= 1 else None for i in range(n_e)]
    s4 = [s2[i] + s2[i - 2] if i >= 3 else None for i in range(n_e)]
    s8 = [s4[i] + s4[i - 4] if i >= 7 else None for i in range(n_e)]
    s16 = [s8[i] + s8[i - 8] if i >= 15 else None for i in range(n_e)]
    grp, win = _pool_lane_consts()
    cnt = jnp.minimum(win, float(PAST_LEN + 1))
    sel = jnp.concatenate([_pool_select(grp, s2[i], s4[i], s8[i], s16[i])
                           for i in range(POOL_BUF, n_e)], axis=0)
    pooled = (sel / cnt - xb).astype(BF16)
    yb = _dot(pooled, poolw_ref[...]) * pools_ref[lrow, :]
    for i in range(POOL_BUF):
        poolst_ref[i] = e[n_e - POOL_BUF + i]

    zc = col(4) * col(5)
    es = [sconvin_ref[i] for i in range(SCONV_W - 1)] + [slab(zc, t) for t in range(DEC_SEQ)]
    yconv = jnp.concatenate(
        [sum(sconvw_ref[layer, pl.ds(k, 1), :] * es[t + k] for k in range(SCONV_W))
         for t in range(DEC_SEQ)], axis=0)
    yc = col(3) * yconv
    for i in range(SCONV_W - 1):
        sconvst_ref[i] = es[len(es) - (SCONV_W - 1) + i]

    zd = col(6) * jax.nn.sigmoid(col(7))
    for i in range(DCONV_W - 1 - DEC_SEQ):
        dconvst_ref[i] = dconvin_ref[i + DEC_SEQ]
    for t in range(DEC_SEQ):
        dconvst_ref[DCONV_W - 1 - DEC_SEQ + t] = slab(zd, t)
    conv = []
    for t in range(DEC_SEQ):
        a = jnp.zeros((nb, G), F32)
        for k in range(DCONV_W):
            i = t + k
            src = dconvin_ref[i] if i < DCONV_W - 1 else slab(zd, i - (DCONV_W - 1))
            a = a + dconvw_ref[layer, pl.ds(k, 1), :] * src
        conv.append(a)
    hd = _ln(jnp.concatenate(conv, axis=0) + dconvb_ref[lrow, :], cvg_ref[lrow, :], cvb_ref[lrow, :])
    yd = _silu(hd)

    mixn = _mixer_norm(ya, yb, yc, yd, onorm_ref, layer)
    o_ref[...] = _mixer_back(x, mixn, mod, wout_ref, lng_ref, lnb_ref, layer)


def _mixer_sample_call(x2d, ada, w_in, w_out, sgu_wsm, sgu_bias, small, pool_in, sconv_in, dconv_in,
                       *, layer):
    (sgu_ln_g, sgu_ln_b, pool_bd, pool_scale, sconv_w, dconv_w, dconv_b, conv_ln_g, conv_ln_b,
     out_norm_g, post_g, post_b) = small
    rows = DEC_SEQ * DEC_BATCH
    full2 = lambda a: _const_spec(a.shape, (0,) * a.ndim)
    st = lambda n: _const_spec((None, n, DEC_BATCH, G), (layer, 0, 0, 0))
    st_out = lambda n: pl.BlockSpec((n, DEC_BATCH, G), lambda i: (0, 0, 0))
    return pl.pallas_call(
        functools.partial(_mixer_sample_kernel, layer=layer),
        grid=(1,),
        in_specs=[
            _const_spec((rows, D_MODEL), (0, 0)),
            _const_spec((None, N_ADA, DEC_BATCH, D_MODEL), (layer, 0, 0, 0)),
            _const_spec((D_MODEL, IN_COLS), (0, 0)),
            _const_spec((D_MODEL, D_MODEL), (0, 0)),
            _const_spec((None, DEC_SEQ, DEC_SEQ, G), (layer, 0, 0, 0)),
            _const_spec((None, DEC_SEQ, G), (layer, 0, 0)),
            full2(sgu_ln_g), full2(sgu_ln_b),
            _const_spec((None, G, G), (layer, 0, 0)),
            full2(pool_scale), full2(sconv_w), full2(dconv_w), full2(dconv_b),
            full2(conv_ln_g), full2(conv_ln_b), full2(out_norm_g), full2(post_g), full2(post_b),
            st(POOL_BUF), st(SCONV_W - 1), st(DCONV_W - 1),
        ],
        out_specs=[pl.BlockSpec((rows, D_MODEL), lambda i: (0, 0)),
                   st_out(DEC_SEQ), st_out(POOL_BUF), st_out(SCONV_W - 1), st_out(DCONV_W - 1)],
        out_shape=[
            jax.ShapeDtypeStruct((rows, D_MODEL), F32),
            jax.ShapeDtypeStruct((DEC_SEQ, DEC_BATCH, G), F32),
            jax.ShapeDtypeStruct((POOL_BUF, DEC_BATCH, G), F32),
            jax.ShapeDtypeStruct((SCONV_W - 1, DEC_BATCH, G), F32),
            jax.ShapeDtypeStruct((DCONV_W - 1, DEC_BATCH, G), F32),
        ],
        compiler_params=pltpu.CompilerParams(
            dimension_semantics=("arbitrary",), vmem_limit_bytes=VMEM_LIMIT_BYTES),
        name=f"mixer_s{layer}",
    )(x2d, ada, w_in, w_out, sgu_wsm, sgu_bias, sgu_ln_g, sgu_ln_b, pool_bd, pool_scale, sconv_w,
      dconv_w, dconv_b, conv_ln_g, conv_ln_b, out_norm_g, post_g, post_b, pool_in, sconv_in, dconv_in)


def kernel(x_prompt, x_sample, state_pool, state_sconv, state_dconv, c_prompt, c_sample, ln_in_g, ln_in_b, w_ada, b_ada, ffn1_w_gate, ffn1_w_up, ffn1_w_down, w_in, sgu_ln_g, sgu_ln_b, sgu_w, sgu_b, pool_w, pool_scale, sconv_w, dconv_w, dconv_b, conv_ln_g, conv_ln_b, out_norm_g, w_out, ffn2_w_gate, ffn2_w_up, ffn2_w_down, post_ln_g, post_ln_b):
    eye = jnp.eye(len(POOL_WINDOWS), dtype=F32)
    pool_bd = jnp.einsum('lgcd,gh->lgchd', pool_w, eye).reshape(DEPTH, G, G).astype(BF16)
    sgu_bias = jnp.repeat(jnp.transpose(sgu_b, (0, 2, 1)), HEAD_DIM, axis=-1)
    sgu_wsm = jnp.repeat(jnp.transpose(sgu_w[:, :, :DEC_SEQ, :DEC_SEQ], (0, 2, 3, 1)),
                         HEAD_DIM, axis=-1)
    in_g, in_b = ln_in_g.reshape(1, D_MODEL), ln_in_b.reshape(1, D_MODEL)
    small = (sgu_ln_g, sgu_ln_b, pool_bd, pool_scale, sconv_w, dconv_w, dconv_b, conv_ln_g,
             conv_ln_b, out_norm_g, post_ln_g, post_ln_b)

    ada = _ada_call(jnp.concatenate([c_sample, c_prompt], axis=0), w_ada, b_ada)

    xs = jnp.transpose(x_sample, (1, 0, 2)).reshape(DEC_SEQ * DEC_BATCH, D_MODEL)
    pool_in = jnp.transpose(state_pool, (0, 2, 1, 3))
    sconv_in = jnp.transpose(state_sconv, (0, 2, 1, 3))
    dconv_in = jnp.transpose(state_dconv, (0, 2, 1, 3))
    xp = x_prompt.reshape(BATCH * SEQ, D_MODEL)

    ffn1 = [w[0].astype(BF16) for w in (ffn1_w_gate, ffn1_w_up, ffn1_w_down)]
    ffn = functools.partial(_ffn_call, post_g=post_ln_g, post_b=post_ln_b, in_g=in_g, in_b=in_b)
    outs_p, outs_s = [], []
    for l in range(DEPTH):
        last = l == DEPTH - 1
        side = [(w_in, l), (w_out, l), (ffn2_w_gate, l), (ffn2_w_up, l), (ffn2_w_down, l)]
        xp, (w_in_b, w_out_b, *ffn2) = ffn(xp, ada, *ffn1, layer=l, sub=0, first=(l == 0), prompt=True,
                                           side=side)
        xs, _ = ffn(xs, ada, *ffn1, layer=l, sub=0, first=(l == 0), prompt=False)
        xp, *st_p = _mixer_prompt_call(xp, ada, w_in_b, w_out_b, sgu_w, sgu_bias, small, layer=l)
        xs, *st_s = _mixer_sample_call(xs, ada, w_in_b, w_out_b, sgu_wsm, sgu_bias, small,
                                       pool_in, sconv_in, dconv_in, layer=l)
        side = [] if last else [(ffn1_w_gate, l + 1), (ffn1_w_up, l + 1), (ffn1_w_down, l + 1)]
        xp, ffn1_next = ffn(xp, ada, *ffn2, layer=l, sub=2, first=False, prompt=True, side=side)
        xs, _ = ffn(xs, ada, *ffn2, layer=l, sub=2, first=False, prompt=False)
        ffn1 = ffn1_next
        outs_p.append(st_p)
        outs_s.append(st_s)

    y_prompt = xp.reshape(BATCH, SEQ, D_MODEL)
    y_sample = jnp.transpose(xs.reshape(DEC_SEQ, DEC_BATCH, D_MODEL), (1, 0, 2))
    stack_p = lambda i: jnp.stack([o[i] for o in outs_p], axis=0)
    stack_s = lambda i: jnp.transpose(jnp.stack([o[i] for o in outs_s], axis=0), (0, 2, 1, 3))
    return (y_prompt, y_sample, stack_p(0), stack_s(0), stack_p(1), stack_s(1),
            stack_p(2), stack_s(2), stack_p(3), stack_s(3))
```

```python
import functools

import jax
import jax.numpy as jnp
from jax import lax
from jax.experimental import pallas as pl
from jax.experimental.pallas import tpu as pltpu

D_MODEL = 1024
BATCH = 8
SEQ = 2048
DEPTH = 2
DEC_BATCH = 128
DEC_SEQ = 8
PAST_LEN = 16384
G = 256
HEAD_DIM = 64
A_HEADS = G // HEAD_DIM
CHUNK = 128
POOL_WINDOWS = (2, 4, 8, 16)
POOL_GDIM = G // len(POOL_WINDOWS)
POOL_BUF = 15
SCONV_W = 3
DCONV_W = 31
D_FF = 2816
N_ADA = 9
ALPHA = (2 * DEPTH) ** 0.25
IN_COLS = 8 * G
N_SEQ_ALL = DEC_BATCH + BATCH
ADA_PER_STEP = 3

SUBLANES = 8
BF16_SUBLANES = 16
POOL_CARRY = 16
SCONV_CARRY = 16
DCONV_CARRY = 32
VMEM_LIMIT_BYTES = 58 * 1024 * 1024

TM_FFN = 1024
ROW_PARTS = 2
TM_MIX = 1024
SUB_MIX = 512

BF16 = jnp.bfloat16
F32 = jnp.float32


def _ln(x, g, b, eps=1e-5):
    mu = jnp.mean(x, axis=-1, keepdims=True)
    xc = x - mu
    var = jnp.mean(xc * xc, axis=-1, keepdims=True)
    return xc * lax.rsqrt(var + eps) * g + b


def _gelu(x):
    return 0.5 * x * (1.0 + lax.erf(x * 0.7071067811865476))


def _silu(x):
    return x * jax.nn.sigmoid(x)


def _rows(x, m):
    if m.shape[0] == 1 or m.shape[0] == x.shape[0]:
        return m
    reps = x.shape[0] // m.shape[0]
    return jnp.broadcast_to(m[None], (reps,) + m.shape).reshape(x.shape)


def _dot(a, b):
    return jnp.dot(a, b, preferred_element_type=F32)


def _const_spec(shape, index):
    return pl.BlockSpec(shape, lambda *_: index, pipeline_mode=pl.Buffered(1))


def _ada_kernel(c_ref, w_ref, b_ref, o_ref):
    s = _silu(c_ref[...]).astype(BF16)
    y = _dot(s, w_ref[...].astype(BF16))
    for k in range(ADA_PER_STEP):
        cols = slice(k * D_MODEL, (k + 1) * D_MODEL)
        o_ref[k] = y[:, cols] + b_ref[pl.ds(pl.program_id(0), 1), cols]


def _ada_call(c_all, w_ada, b_ada):
    return pl.pallas_call(
        _ada_kernel,
        grid=(DEPTH, N_ADA // ADA_PER_STEP),
        in_specs=[
            pl.BlockSpec((N_SEQ_ALL, D_MODEL), lambda l, k: (0, 0)),
            pl.BlockSpec((None, D_MODEL, ADA_PER_STEP * D_MODEL), lambda l, k: (l, 0, k)),
            pl.BlockSpec((DEPTH, ADA_PER_STEP * D_MODEL), lambda l, k: (0, k)),
        ],
        out_specs=pl.BlockSpec((None, ADA_PER_STEP, N_SEQ_ALL, D_MODEL), lambda l, k: (l, k, 0, 0)),
        out_shape=jax.ShapeDtypeStruct((DEPTH, N_ADA, N_SEQ_ALL, D_MODEL), F32),
        compiler_params=pltpu.CompilerParams(
            dimension_semantics=("arbitrary", "arbitrary"), vmem_limit_bytes=VMEM_LIMIT_BYTES),
        name="ada",
    )(c_all, w_ada, b_ada)


def _ffn_math(x, mod, wg_ref, wu_ref, wd_ref, lng_ref, lnb_ref, *, layer, sub, arrive=None):
    k0 = 0 if sub == 0 else 6
    sh, sc, gt = mod(k0), mod(k0 + 1), mod(k0 + 2)
    xm = (x * (1.0 + _rows(x, sc)) + _rows(x, sh)).astype(BF16)
    if arrive:
        arrive[0]()
    hg = _dot(xm, wg_ref[...])
    if arrive:
        arrive[1]()
    hu = _dot(xm, wu_ref[...])
    if arrive:
        arrive[2]()
    h = (_silu(hg) * hu).astype(BF16)
    gain = 0.5 * (1.0 + _rows(x, gt))
    half = x.shape[0] // 2
    outs = []
    for r in (slice(0, half), slice(half, 2 * half)):
        y = ALPHA * x[r] + (gain if gain.shape[0] == 1 else gain[r]) * _dot(h[r], wd_ref[...])
        outs.append(_ln(y, lng_ref[layer, pl.ds(sub, 1), :], lnb_ref[layer, pl.ds(sub, 1), :]))
    return jnp.concatenate(outs, axis=0)


def _ffn_kernel(x_ref, ada_ref, wg_hbm, wu_hbm, wd_hbm, lng_ref, lnb_ref, ing_ref, inb_ref, *rest,
                layer, sub, first, prompt, n_side):
    side_in, o_ref, side_out = rest[:n_side], rest[n_side], rest[n_side + 1:n_side + 1 + n_side]
    wg_ref, wu_ref, wd_ref, sems = rest[n_side + 1 + n_side:]
    if prompt:
        b = pl.program_id(0)
        mod = lambda k: ada_ref[k, pl.ds(b, 1), :]
    else:
        mod = lambda k: ada_ref[k]
    n_part = ROW_PARTS if prompt else 1
    part = x_ref.shape[0] // n_part

    def half_step(arrive):
        for i in range(n_part):
            rows = slice(i * part, (i + 1) * part)
            x = x_ref[rows, :]
            if first:
                x = _ln(x, ing_ref[...], inb_ref[...])
            o_ref[rows, :] = _ffn_math(x, mod, wg_ref, wu_ref, wd_ref, lng_ref, lnb_ref, layer=layer,
                                       sub=sub, arrive=arrive if i == 0 else None)

    copies = [pltpu.make_async_copy(src, dst, sems.at[k])
              for k, (src, dst) in enumerate(((wg_hbm, wg_ref), (wu_hbm, wu_ref), (wd_hbm, wd_ref)))]
    first_step = jnp.logical_and(pl.program_id(0) == 0, pl.program_id(1) == 0)

    @pl.when(first_step)
    def _():
        copies[0].start()

        def gate_arrives():
            copies[0].wait()
            copies[1].start()
            copies[2].start()

        half_step((gate_arrives, copies[1].wait, copies[2].wait))

    @pl.when(jnp.logical_not(first_step))
    def _():
        half_step(None)

    for src, dst in zip(side_in, side_out):
        dst[...] = src[...].astype(BF16)


def _ffn_call(x2d, ada, wg, wu, wd, post_g, post_b, in_g, in_b, *, layer, sub, first, prompt, side=()):
    rows = x2d.shape[0]
    if prompt:
        nt = SEQ // TM_FFN
        grid = (BATCH, nt)
        x_spec = pl.BlockSpec((TM_FFN, D_MODEL), lambda b, j: (b * nt + j, 0))
        ada_spec = _const_spec((None, N_ADA, BATCH, D_MODEL), (layer, 0, DEC_BATCH // BATCH, 0))
    else:
        assert not side
        grid = (1, 1)
        x_spec = pl.BlockSpec((rows, D_MODEL), lambda b, j: (0, 0))
        ada_spec = _const_spec((None, N_ADA, DEC_BATCH, D_MODEL), (layer, 0, 0, 0))
    side_in_specs, side_out_specs, side_shapes = [], [], []
    for w, wl in side:
        k_dim, n_dim = w.shape[1:]
        steps = grid[0] * grid[1]
        while (k_dim // steps) % BF16_SUBLANES:
            steps //= 2
        per = grid[0] * grid[1] // steps
        chunk = lambda b, j, per=per: (b * grid[1] + j) // per
        side_in_specs.append(pl.BlockSpec((None, k_dim // steps, n_dim),
                                          lambda b, j, wl=wl, chunk=chunk: (wl, chunk(b, j), 0)))
        side_out_specs.append(pl.BlockSpec((k_dim // steps, n_dim),
                                           lambda b, j, chunk=chunk: (chunk(b, j), 0)))
        side_shapes.append(jax.ShapeDtypeStruct((k_dim, n_dim), BF16))
    kern = functools.partial(_ffn_kernel, layer=layer, sub=sub, first=first, prompt=prompt,
                             n_side=len(side))
    res = pl.pallas_call(
        kern,
        grid=grid,
        in_specs=[
            x_spec, ada_spec,
            pl.BlockSpec(memory_space=pl.ANY),
            pl.BlockSpec(memory_space=pl.ANY),
            pl.BlockSpec(memory_space=pl.ANY),
            _const_spec((DEPTH, 3, D_MODEL), (0, 0, 0)),
            _const_spec((DEPTH, 3, D_MODEL), (0, 0, 0)),
            _const_spec((1, D_MODEL), (0, 0)),
            _const_spec((1, D_MODEL), (0, 0)),
        ] + side_in_specs,
        out_specs=[x_spec] + side_out_specs,
        out_shape=[jax.ShapeDtypeStruct((rows, D_MODEL), F32)] + side_shapes,
        scratch_shapes=[
            pltpu.VMEM((D_MODEL, D_FF), BF16),
            pltpu.VMEM((D_MODEL, D_FF), BF16),
            pltpu.VMEM((D_FF, D_MODEL), BF16),
            pltpu.SemaphoreType.DMA((3,)),
        ],
        compiler_params=pltpu.CompilerParams(
            dimension_semantics=("arbitrary", "arbitrary"), vmem_limit_bytes=VMEM_LIMIT_BYTES),
        name=f"ffn{sub}_{'p' if prompt else 's'}{layer}",
    )(x2d, ada, wg, wu, wd, post_g, post_b, in_g, in_b, *[w for w, _ in side])
    return res[0], list(res[1:])


def _pool_lane_consts():
    lane = lax.broadcasted_iota(jnp.int32, (1, G), 1)
    grp = lane // POOL_GDIM
    win = jnp.where(grp == 0, float(POOL_WINDOWS[0]),
                    jnp.where(grp == 1, float(POOL_WINDOWS[1]),
                              jnp.where(grp == 2, float(POOL_WINDOWS[2]), float(POOL_WINDOWS[3]))))
    return grp, win


def _pool_select(grp, s2, s4, s8, s16):
    return jnp.where(grp == 0, s2, jnp.where(grp == 1, s4, jnp.where(grp == 2, s8, s16)))


def _rms_group(y, g, eps=1e-6):
    return y * lax.rsqrt(jnp.mean(y * y, axis=-1, keepdims=True) + eps) * g


def _mixer_front(x, mod, win_ref):
    sh, sc = mod(3), mod(4)
    xm = (x * (1.0 + _rows(x, sc)) + _rows(x, sh)).astype(BF16)
    return _dot(xm, win_ref[...])


def _mixer_norm(ya, yb, yc, yd, onorm_ref, layer):
    parts = [ya, yb, yc, yd]
    normed = [_rms_group(p, onorm_ref[pl.ds(layer, 1), i * G:(i + 1) * G]).astype(BF16)
              for i, p in enumerate(parts)]
    return jnp.concatenate(normed, axis=-1)


def _mixer_back(x, mixn, mod, wout_ref, lng_ref, lnb_ref, layer):
    y = ALPHA * x + (1.0 + _rows(x, mod(5))) * _dot(mixn, wout_ref[...])
    return _ln(y, lng_ref[layer, pl.ds(1, 1), :], lnb_ref[layer, pl.ds(1, 1), :])


def _mixer_prompt_mid(proj, pos0, r0, last, sguw_ref, sgub_ref, sgug_ref, sgulb_ref,
                      poolw_ref, pools_ref, sconvw_ref, dconvw_ref, dconvb_ref, cvg_ref, cvb_ref,
                      onorm_ref, sguv_ref, poolst_ref, sconvst_ref, dconvst_ref,
                      pool_ext, sconv_ext, dconv_ext, *, layer):
    n = proj.shape[0]
    lrow = pl.ds(layer, 1)
    col = lambda i: proj[:, i * G:(i + 1) * G]

    u = _gelu(col(0))
    v = _ln(_gelu(col(1)), sgug_ref[lrow, :], sgulb_ref[lrow, :])
    if last:
        sguv_ref[...] = v[n - CHUNK:, :]
    r_i = lax.broadcasted_iota(jnp.int32, (CHUNK, CHUNK), 0)
    c_i = lax.broadcasted_iota(jnp.int32, (CHUNK, CHUNK), 1)
    wcat = jnp.concatenate([jnp.where(r_i >= c_i, sguw_ref[h], 0.0) for h in range(A_HEADS)],
                           axis=1).astype(BF16)
    head = lax.broadcasted_iota(jnp.int32, (1, G), 1) // HEAD_DIM
    bias = sgub_ref[...]
    mixed = []
    for c in range(n // CHUNK):
        vc = v[c * CHUNK:(c + 1) * CHUNK, :]
        rhs = jnp.concatenate([jnp.where(head == h, vc, 0.0) for h in range(A_HEADS)],
                              axis=0).astype(BF16)
        mixed.append(_dot(wcat, rhs) + bias)
    ya = u * jnp.concatenate(mixed, axis=0)

    xb = col(2)
    pool_ext[r0 + POOL_CARRY:r0 + POOL_CARRY + n, :] = xb
    e = pool_ext[r0:r0 + POOL_CARRY + n, :]
    s2 = e + pltpu.roll(e, 1, 0)
    s4 = s2 + pltpu.roll(s2, 2, 0)
    s8 = s4 + pltpu.roll(s4, 4, 0)
    s16 = s8 + pltpu.roll(s8, 8, 0)
    grp, win = _pool_lane_consts()
    ssel = _pool_select(grp, s2, s4, s8, s16)[POOL_CARRY:, :]
    pos = pos0 + lax.broadcasted_iota(jnp.int32, (n, 1), 0)
    cnt = jnp.minimum(win, (pos + 1).astype(F32))
    pooled = (ssel / cnt - xb).astype(BF16)
    yb = _dot(pooled, poolw_ref[...]) * pools_ref[lrow, :]
    if last:
        poolst_ref[...] = pool_ext[r0 + n + POOL_CARRY - POOL_BUF:r0 + n + POOL_CARRY, :]

    zc = col(4) * col(5)
    sconv_ext[r0 + SCONV_CARRY:r0 + SCONV_CARRY + n, :] = zc
    es = sconv_ext[r0:r0 + SCONV_CARRY + n, :]
    yconv = (sconvw_ref[layer, pl.ds(2, 1), :] * es
             + sconvw_ref[layer, pl.ds(1, 1), :] * pltpu.roll(es, 1, 0)
             + sconvw_ref[layer, pl.ds(0, 1), :] * pltpu.roll(es, 2, 0))
    yc = col(3) * yconv[SCONV_CARRY:, :]
    if last:
        sconvst_ref[...] = sconv_ext[r0 + n + SCONV_CARRY - (SCONV_W - 1):r0 + n + SCONV_CARRY, :]

    zd = col(6) * jax.nn.sigmoid(col(7))
    dconv_ext[r0 + DCONV_CARRY:r0 + DCONV_CARRY + n, :] = zd
    if last:
        dconvst_ref[...] = dconv_ext[r0 + n + DCONV_CARRY - (DCONV_W - 1):r0 + n + DCONV_CARRY, :]
    ed = dconv_ext[r0:r0 + DCONV_CARRY + n, :]
    acc = jnp.zeros((n, G), F32)
    for r in range(SUBLANES):
        er = ed if r == 0 else pltpu.roll(ed, r, 0)
        for q in range(DCONV_CARRY // SUBLANES):
            tap = SUBLANES * q + r
            if tap >= DCONV_W:
                continue
            lo = DCONV_CARRY - SUBLANES * q
            acc = acc + dconvw_ref[layer, pl.ds(DCONV_W - 1 - tap, 1), :] * er[lo:lo + n, :]
    hd = _ln(acc + dconvb_ref[lrow, :], cvg_ref[lrow, :], cvb_ref[lrow, :])
    yd = _silu(hd)

    return _mixer_norm(ya, yb, yc, yd, onorm_ref, layer)


def _mixer_prompt_kernel(x_ref, ada_ref, win_ref, wout_ref, sguw_ref, sgub_ref, sgug_ref, sgulb_ref,
                         poolw_ref, pools_ref, sconvw_ref, dconvw_ref, dconvb_ref, cvg_ref, cvb_ref,
                         onorm_ref, lng_ref, lnb_ref,
                         o_ref, sguv_ref, poolst_ref, sconvst_ref, dconvst_ref,
                         pool_ext, sconv_ext, dconv_ext, *, layer, sub_rows):
    tm = x_ref.shape[0]
    b = pl.program_id(0)
    j = pl.program_id(1)

    @pl.when(j == 0)
    def _():
        pool_ext[0:POOL_CARRY, :] = jnp.zeros((POOL_CARRY, G), F32)
        sconv_ext[0:SCONV_CARRY, :] = jnp.zeros((SCONV_CARRY, G), F32)
        dconv_ext[0:DCONV_CARRY, :] = jnp.zeros((DCONV_CARRY, G), F32)

    @pl.when(j > 0)
    def _():
        pool_ext[0:POOL_CARRY, :] = pool_ext[tm:tm + POOL_CARRY, :]
        sconv_ext[0:SCONV_CARRY, :] = sconv_ext[tm:tm + SCONV_CARRY, :]
        dconv_ext[0:DCONV_CARRY, :] = dconv_ext[tm:tm + DCONV_CARRY, :]

    mod = lambda k: ada_ref[k, pl.ds(b, 1), :]
    n_sub = tm // sub_rows
    rows = lambda h: slice(h * sub_rows, (h + 1) * sub_rows)
    front = lambda h: _mixer_front(x_ref[rows(h), :], mod, win_ref)
    proj = front(0)
    for h in range(n_sub):
        proj_next = front(h + 1) if h + 1 < n_sub else None
        mixn = _mixer_prompt_mid(
            proj, j * tm + h * sub_rows, h * sub_rows, h == n_sub - 1,
            sguw_ref, sgub_ref, sgug_ref, sgulb_ref, poolw_ref, pools_ref, sconvw_ref,
            dconvw_ref, dconvb_ref, cvg_ref, cvb_ref, onorm_ref,
            sguv_ref, poolst_ref, sconvst_ref, dconvst_ref, pool_ext, sconv_ext, dconv_ext, layer=layer)
        o_ref[rows(h), :] = _mixer_back(x_ref[rows(h), :], mixn, mod, wout_ref, lng_ref, lnb_ref, layer)
        proj = proj_next


def _mixer_prompt_call(x2d, ada, w_in, w_out, sgu_w, sgu_bias, small, *, layer):
    (sgu_ln_g, sgu_ln_b, pool_bd, pool_scale, sconv_w, dconv_w, dconv_b, conv_ln_g, conv_ln_b,
     out_norm_g, post_g, post_b) = small
    tm = TM_MIX
    nt = SEQ // tm
    x_spec = pl.BlockSpec((tm, D_MODEL), lambda b, j: (b * nt + j, 0))
    st_spec = lambda n: pl.BlockSpec((None, n, G), lambda b, j: (b, 0, 0))
    full2 = lambda a: _const_spec(a.shape, (0,) * a.ndim)
    return pl.pallas_call(
        functools.partial(_mixer_prompt_kernel, layer=layer, sub_rows=SUB_MIX),
        grid=(BATCH, nt),
        in_specs=[
            x_spec,
            _const_spec((None, N_ADA, BATCH, D_MODEL), (layer, 0, DEC_BATCH // BATCH, 0)),
            _const_spec((D_MODEL, IN_COLS), (0, 0)),
            _const_spec((D_MODEL, D_MODEL), (0, 0)),
            _const_spec((None, A_HEADS, CHUNK, CHUNK), (layer, 0, 0, 0)),
            _const_spec((None, CHUNK, G), (layer, 0, 0)),
            full2(sgu_ln_g), full2(sgu_ln_b),
            _const_spec((None, G, G), (layer, 0, 0)),
            full2(pool_scale), full2(sconv_w), full2(dconv_w), full2(dconv_b),
            full2(conv_ln_g), full2(conv_ln_b), full2(out_norm_g), full2(post_g), full2(post_b),
        ],
        out_specs=[x_spec, st_spec(CHUNK), st_spec(POOL_BUF), st_spec(SCONV_W - 1), st_spec(DCONV_W - 1)],
        out_shape=[
            jax.ShapeDtypeStruct((BATCH * SEQ, D_MODEL), F32),
            jax.ShapeDtypeStruct((BATCH, CHUNK, G), F32),
            jax.ShapeDtypeStruct((BATCH, POOL_BUF, G), F32),
            jax.ShapeDtypeStruct((BATCH, SCONV_W - 1, G), F32),
            jax.ShapeDtypeStruct((BATCH, DCONV_W - 1, G), F32),
        ],
        scratch_shapes=[
            pltpu.VMEM((tm + POOL_CARRY, G), F32),
            pltpu.VMEM((tm + SCONV_CARRY, G), F32),
            pltpu.VMEM((tm + DCONV_CARRY, G), F32),
        ],
        compiler_params=pltpu.CompilerParams(
            dimension_semantics=("arbitrary", "arbitrary"), vmem_limit_bytes=VMEM_LIMIT_BYTES),
        name=f"mixer_p{layer}",
    )(x2d, ada, w_in, w_out, sgu_w, sgu_bias, sgu_ln_g, sgu_ln_b, pool_bd, pool_scale, sconv_w,
      dconv_w, dconv_b, conv_ln_g, conv_ln_b, out_norm_g, post_g, post_b)


def _mixer_sample_kernel(x_ref, ada_ref, win_ref, wout_ref, sguw_ref, sgub_ref, sgug_ref, sgulb_ref,
                         poolw_ref, pools_ref, sconvw_ref, dconvw_ref, dconvb_ref, cvg_ref, cvb_ref,
                         onorm_ref, lng_ref, lnb_ref, poolin_ref, sconvin_ref, dconvin_ref,
                         o_ref, sguv_ref, poolst_ref, sconvst_ref, dconvst_ref, *, layer):
    nb = DEC_BATCH
    lrow = pl.ds(layer, 1)
    x = x_ref[...]
    mod = lambda k: ada_ref[k]
    proj = _mixer_front(x, mod, win_ref)
    col = lambda i: proj[:, i * G:(i + 1) * G]
    slab = lambda a, t: a[t * nb:(t + 1) * nb, :]

    u = _gelu(col(0))
    v = _ln(_gelu(col(1)), sgug_ref[lrow, :], sgulb_ref[lrow, :])
    for t in range(DEC_SEQ):
        sguv_ref[t] = slab(v, t)
    mixed = []
    for t in range(DEC_SEQ):
        m = jnp.zeros((nb, G), F32) + sgub_ref[pl.ds(t, 1), :]
        for s in range(t + 1):
            m = m + sguw_ref[t, pl.ds(s, 1), :] * slab(v, s)
        mixed.append(m)
    ya = u * jnp.concatenate(mixed, axis=0)

    xb = col(2)
    e = [poolin_ref[i] for i in range(POOL_BUF)] + [slab(xb, t) for t in range(DEC_SEQ)]
    n_e = len(e)
    s2 = [e[i] + e[i - 1] if i >= 1 else None for i in range(n_e)]
    s4 = [s2[i] + s2[i - 2] if i >= 3 else None for i in range(n_e)]
    s8 = [s4[i] + s4[i - 4] if i >= 7 else None for i in range(n_e)]
    s16 = [s8[i] + s8[i - 8] if i >= 15 else None for i in range(n_e)]
    grp, win = _pool_lane_consts()
    cnt = jnp.minimum(win, float(PAST_LEN + 1))
    sel = jnp.concatenate([_pool_select(grp, s2[i], s4[i], s8[i], s16[i])
                           for i in range(POOL_BUF, n_e)], axis=0)
    pooled = (sel / cnt - xb).astype(BF16)
    yb = _dot(pooled, poolw_ref[...]) * pools_ref[lrow, :]
    for i in range(POOL_BUF):
        poolst_ref[i] = e[n_e - POOL_BUF + i]

    zc = col(4) * col(5)
    es = [sconvin_ref[i] for i in range(SCONV_W - 1)] + [slab(zc, t) for t in range(DEC_SEQ)]
    yconv = jnp.concatenate(
        [sum(sconvw_ref[layer, pl.ds(k, 1), :] * es[t + k] for k in range(SCONV_W))
         for t in range(DEC_SEQ)], axis=0)
    yc = col(3) * yconv
    for i in range(SCONV_W - 1):
        sconvst_ref[i] = es[len(es) - (SCONV_W - 1) + i]

    zd = col(6) * jax.nn.sigmoid(col(7))
    for i in range(DCONV_W - 1 - DEC_SEQ):
        dconvst_ref[i] = dconvin_ref[i + DEC_SEQ]
    for t in range(DEC_SEQ):
        dconvst_ref[DCONV_W - 1 - DEC_SEQ + t] = slab(zd, t)
    conv = []
    for t in range(DEC_SEQ):
        a = jnp.zeros((nb, G), F32)
        for k in range(DCONV_W):
            i = t + k
            src = dconvin_ref[i] if i < DCONV_W - 1 else slab(zd, i - (DCONV_W - 1))
            a = a + dconvw_ref[layer, pl.ds(k, 1), :] * src
        conv.append(a)
    hd = _ln(jnp.concatenate(conv, axis=0) + dconvb_ref[lrow, :], cvg_ref[lrow, :], cvb_ref[lrow, :])
    yd = _silu(hd)

    mixn = _mixer_norm(ya, yb, yc, yd, onorm_ref, layer)
    o_ref[...] = _mixer_back(x, mixn, mod, wout_ref, lng_ref, lnb_ref, layer)


def _mixer_sample_call(x2d, ada, w_in, w_out, sgu_wsm, sgu_bias, small, pool_in, sconv_in, dconv_in,
                       *, layer):
    (sgu_ln_g, sgu_ln_b, pool_bd, pool_scale, sconv_w, dconv_w, dconv_b, conv_ln_g, conv_ln_b,
     out_norm_g, post_g, post_b) = small
    rows = DEC_SEQ * DEC_BATCH
    full2 = lambda a: _const_spec(a.shape, (0,) * a.ndim)
    st = lambda n: _const_spec((None, n, DEC_BATCH, G), (layer, 0, 0, 0))
    st_out = lambda n: pl.BlockSpec((n, DEC_BATCH, G), lambda i: (0, 0, 0))
    return pl.pallas_call(
        functools.partial(_mixer_sample_kernel, layer=layer),
        grid=(1,),
        in_specs=[
            _const_spec((rows, D_MODEL), (0, 0)),
            _const_spec((None, N_ADA, DEC_BATCH, D_MODEL), (layer, 0, 0, 0)),
            _const_spec((D_MODEL, IN_COLS), (0, 0)),
            _const_spec((D_MODEL, D_MODEL), (0, 0)),
            _const_spec((None, DEC_SEQ, DEC_SEQ, G), (layer, 0, 0, 0)),
            _const_spec((None, DEC_SEQ, G), (layer, 0, 0)),
            full2(sgu_ln_g), full2(sgu_ln_b),
            _const_spec((None, G, G), (layer, 0, 0)),
            full2(pool_scale), full2(sconv_w), full2(dconv_w), full2(dconv_b),
            full2(conv_ln_g), full2(conv_ln_b), full2(out_norm_g), full2(post_g), full2(post_b),
            st(POOL_BUF), st(SCONV_W - 1), st(DCONV_W - 1),
        ],
        out_specs=[pl.BlockSpec((rows, D_MODEL), lambda i: (0, 0)),
                   st_out(DEC_SEQ), st_out(POOL_BUF), st_out(SCONV_W - 1), st_out(DCONV_W - 1)],
        out_shape=[
            jax.ShapeDtypeStruct((rows, D_MODEL), F32),
            jax.ShapeDtypeStruct((DEC_SEQ, DEC_BATCH, G), F32),
            jax.ShapeDtypeStruct((POOL_BUF, DEC_BATCH, G), F32),
            jax.ShapeDtypeStruct((SCONV_W - 1, DEC_BATCH, G), F32),
            jax.ShapeDtypeStruct((DCONV_W - 1, DEC_BATCH, G), F32),
        ],
        compiler_params=pltpu.CompilerParams(
            dimension_semantics=("arbitrary",), vmem_limit_bytes=VMEM_LIMIT_BYTES),
        name=f"mixer_s{layer}",
    )(x2d, ada, w_in, w_out, sgu_wsm, sgu_bias, sgu_ln_g, sgu_ln_b, pool_bd, pool_scale, sconv_w,
      dconv_w, dconv_b, conv_ln_g, conv_ln_b, out_norm_g, post_g, post_b, pool_in, sconv_in, dconv_in)


def kernel(x_prompt, x_sample, state_pool, state_sconv, state_dconv, c_prompt, c_sample, ln_in_g, ln_in_b, w_ada, b_ada, ffn1_w_gate, ffn1_w_up, ffn1_w_down, w_in, sgu_ln_g, sgu_ln_b, sgu_w, sgu_b, pool_w, pool_scale, sconv_w, dconv_w, dconv_b, conv_ln_g, conv_ln_b, out_norm_g, w_out, ffn2_w_gate, ffn2_w_up, ffn2_w_down, post_ln_g, post_ln_b):
    eye = jnp.eye(len(POOL_WINDOWS), dtype=F32)
    pool_bd = jnp.einsum('lgcd,gh->lgchd', pool_w, eye).reshape(DEPTH, G, G).astype(BF16)
    sgu_bias = jnp.repeat(jnp.transpose(sgu_b, (0, 2, 1)), HEAD_DIM, axis=-1)
    sgu_wsm = jnp.repeat(jnp.transpose(sgu_w[:, :, :DEC_SEQ, :DEC_SEQ], (0, 2, 3, 1)),
                         HEAD_DIM, axis=-1)
    in_g, in_b = ln_in_g.reshape(1, D_MODEL), ln_in_b.reshape(1, D_MODEL)
    small = (sgu_ln_g, sgu_ln_b, pool_bd, pool_scale, sconv_w, dconv_w, dconv_b, conv_ln_g,
             conv_ln_b, out_norm_g, post_ln_g, post_ln_b)

    ada = _ada_call(jnp.concatenate([c_sample, c_prompt], axis=0), w_ada, b_ada)

    xs = jnp.transpose(x_sample, (1, 0, 2)).reshape(DEC_SEQ * DEC_BATCH, D_MODEL)
    pool_in = jnp.transpose(state_pool, (0, 2, 1, 3))
    sconv_in = jnp.transpose(state_sconv, (0, 2, 1, 3))
    dconv_in = jnp.transpose(state_dconv, (0, 2, 1, 3))
    xp = x_prompt.reshape(BATCH * SEQ, D_MODEL)

    ffn1 = [w[0].astype(BF16) for w in (ffn1_w_gate, ffn1_w_up, ffn1_w_down)]
    ffn = functools.partial(_ffn_call, post_g=post_ln_g, post_b=post_ln_b, in_g=in_g, in_b=in_b)
    outs_p, outs_s = [], []
    for l in range(DEPTH):
        last = l == DEPTH - 1
        side = [(w_in, l), (w_out, l), (ffn2_w_gate, l), (ffn2_w_up, l), (ffn2_w_down, l)]
        xp, (w_in_b, w_out_b, *ffn2) = ffn(xp, ada, *ffn1, layer=l, sub=0, first=(l == 0), prompt=True,
                                           side=side)
        xs, _ = ffn(xs, ada, *ffn1, layer=l, sub=0, first=(l == 0), prompt=False)
        xp, *st_p = _mixer_prompt_call(xp, ada, w_in_b, w_out_b, sgu_w, sgu_bias, small, layer=l)
        xs, *st_s = _mixer_sample_call(xs, ada, w_in_b, w_out_b, sgu_wsm, sgu_bias, small,
                                       pool_in, sconv_in, dconv_in, layer=l)
        side = [] if last else [(ffn1_w_gate, l + 1), (ffn1_w_up, l + 1), (ffn1_w_down, l + 1)]
        xp, ffn1_next = ffn(xp, ada, *ffn2, layer=l, sub=2, first=False, prompt=True, side=side)
        xs, _ = ffn(xs, ada, *ffn2, layer=l, sub=2, first=False, prompt=False)
        ffn1 = ffn1_next
        outs_p.append(st_p)
        outs_s.append(st_s)

    y_prompt = xp.reshape(BATCH, SEQ, D_MODEL)
    y_sample = jnp.transpose(xs.reshape(DEC_SEQ, DEC_BATCH, D_MODEL), (1, 0, 2))
    stack_p = lambda i: jnp.stack([o[i] for o in outs_p], axis=0)
    stack_s = lambda i: jnp.transpose(jnp.stack([o[i] for o in outs_s], axis=0), (0, 2, 1, 3))
    return (y_prompt, y_sample, stack_p(0), stack_s(0), stack_p(1), stack_s(1),
            stack_p(2), stack_s(2), stack_p(3), stack_s(3))
```

```python
import functools

import jax
import jax.numpy as jnp
from jax import lax
from jax.experimental import pallas as pl
from jax.experimental.pallas import tpu as pltpu

D_MODEL = 1024
BATCH = 8
SEQ = 2048
DEPTH = 2
DEC_BATCH = 128
DEC_SEQ = 8
PAST_LEN = 16384
G = 256
HEAD_DIM = 64
A_HEADS = G // HEAD_DIM
CHUNK = 128
POOL_WINDOWS = (2, 4, 8, 16)
POOL_GDIM = G // len(POOL_WINDOWS)
POOL_BUF = 15
SCONV_W = 3
DCONV_W = 31
D_FF = 2816
N_ADA = 9
ALPHA = (2 * DEPTH) ** 0.25
IN_COLS = 8 * G
N_SEQ_ALL = DEC_BATCH + BATCH
ADA_PER_STEP = 3

SUBLANES = 8
BF16_SUBLANES = 16
POOL_CARRY = 16
SCONV_CARRY = 16
DCONV_CARRY = 32
VMEM_LIMIT_BYTES = 56 * 1024 * 1024

TM_FFN = 1024
ROW_PARTS = 2
TM_MIX = 1024
SUB_MIX = 512

BF16 = jnp.bfloat16
F32 = jnp.float32


def _ln(x, g, b, eps=1e-5):
    mu = jnp.mean(x, axis=-1, keepdims=True)
    xc = x - mu
    var = jnp.mean(xc * xc, axis=-1, keepdims=True)
    return xc * lax.rsqrt(var + eps) * g + b


def _gelu(x):
    return 0.5 * x * (1.0 + lax.erf(x * 0.7071067811865476))


def _silu(x):
    return x * jax.nn.sigmoid(x)


def _rows(x, m):
    if m.shape[0] == 1 or m.shape[0] == x.shape[0]:
        return m
    reps = x.shape[0] // m.shape[0]
    return jnp.broadcast_to(m[None], (reps,) + m.shape).reshape(x.shape)


def _dot(a, b):
    return jnp.dot(a, b, preferred_element_type=F32)


def _const_spec(shape, index):
    return pl.BlockSpec(shape, lambda *_: index, pipeline_mode=pl.Buffered(1))


def _ada_kernel(c_ref, w_ref, b_ref, o_ref):
    s = _silu(c_ref[...]).astype(BF16)
    y = _dot(s, w_ref[...].astype(BF16))
    for k in range(ADA_PER_STEP):
        cols = slice(k * D_MODEL, (k + 1) * D_MODEL)
        o_ref[k] = y[:, cols] + b_ref[pl.ds(pl.program_id(0), 1), cols]


def _ada_call(c_all, w_ada, b_ada):
    return pl.pallas_call(
        _ada_kernel,
        grid=(DEPTH, N_ADA // ADA_PER_STEP),
        in_specs=[
            pl.BlockSpec((N_SEQ_ALL, D_MODEL), lambda l, k: (0, 0)),
            pl.BlockSpec((None, D_MODEL, ADA_PER_STEP * D_MODEL), lambda l, k: (l, 0, k)),
            pl.BlockSpec((DEPTH, ADA_PER_STEP * D_MODEL), lambda l, k: (0, k)),
        ],
        out_specs=pl.BlockSpec((None, ADA_PER_STEP, N_SEQ_ALL, D_MODEL), lambda l, k: (l, k, 0, 0)),
        out_shape=jax.ShapeDtypeStruct((DEPTH, N_ADA, N_SEQ_ALL, D_MODEL), F32),
        compiler_params=pltpu.CompilerParams(
            dimension_semantics=("arbitrary", "arbitrary"), vmem_limit_bytes=VMEM_LIMIT_BYTES),
        name="ada",
    )(c_all, w_ada, b_ada)


def _ffn_math(x, mod, wg_ref, wu_ref, wd_ref, lng_ref, lnb_ref, *, layer, sub):
    k0 = 0 if sub == 0 else 6
    sh, sc, gt = mod(k0), mod(k0 + 1), mod(k0 + 2)
    xm = (x * (1.0 + _rows(x, sc)) + _rows(x, sh)).astype(BF16)
    hg = _dot(xm, wg_ref[...])
    hu = _dot(xm, wu_ref[...])
    h = (_silu(hg) * hu).astype(BF16)
    gain = 0.5 * (1.0 + _rows(x, gt))
    half = x.shape[0] // 2
    outs = []
    for r in (slice(0, half), slice(half, 2 * half)):
        y = ALPHA * x[r] + (gain if gain.shape[0] == 1 else gain[r]) * _dot(h[r], wd_ref[...])
        outs.append(_ln(y, lng_ref[layer, pl.ds(sub, 1), :], lnb_ref[layer, pl.ds(sub, 1), :]))
    return jnp.concatenate(outs, axis=0)


def _ffn_kernel(x_ref, ada_ref, wg_ref, wu_ref, wd_ref, lng_ref, lnb_ref, ing_ref, inb_ref, *rest,
                layer, sub, first, prompt, n_side):
    side_in, o_ref, side_out = rest[:n_side], rest[n_side], rest[n_side + 1:]
    if prompt:
        b = pl.program_id(0)
        mod = lambda k: ada_ref[k, pl.ds(b, 1), :]
    else:
        mod = lambda k: ada_ref[k]
    n_part = ROW_PARTS if prompt else 1
    part = x_ref.shape[0] // n_part
    for i in range(n_part):
        rows = slice(i * part, (i + 1) * part)
        x = x_ref[rows, :]
        if first:
            x = _ln(x, ing_ref[...], inb_ref[...])
        o_ref[rows, :] = _ffn_math(x, mod, wg_ref, wu_ref, wd_ref, lng_ref, lnb_ref, layer=layer, sub=sub)
    for src, dst in zip(side_in, side_out):
        dst[...] = src[...].astype(BF16)


def _ffn_call(x2d, ada, wg, wu, wd, post_g, post_b, in_g, in_b, *, layer, sub, first, prompt, side=()):
    rows = x2d.shape[0]
    if prompt:
        nt = SEQ // TM_FFN
        grid = (BATCH, nt)
        x_spec = pl.BlockSpec((TM_FFN, D_MODEL), lambda b, j: (b * nt + j, 0))
        ada_spec = _const_spec((None, N_ADA, BATCH, D_MODEL), (layer, 0, DEC_BATCH // BATCH, 0))
    else:
        assert not side
        grid = (1, 1)
        x_spec = pl.BlockSpec((rows, D_MODEL), lambda b, j: (0, 0))
        ada_spec = _const_spec((None, N_ADA, DEC_BATCH, D_MODEL), (layer, 0, 0, 0))
    side_in_specs, side_out_specs, side_shapes = [], [], []
    for w, wl in side:
        k_dim, n_dim = w.shape[1:]
        steps = grid[0] * grid[1]
        while (k_dim // steps) % BF16_SUBLANES:
            steps //= 2
        per = grid[0] * grid[1] // steps
        chunk = lambda b, j, per=per: (b * grid[1] + j) // per
        side_in_specs.append(pl.BlockSpec((None, k_dim // steps, n_dim),
                                          lambda b, j, wl=wl, chunk=chunk: (wl, chunk(b, j), 0)))
        side_out_specs.append(pl.BlockSpec((k_dim // steps, n_dim),
                                           lambda b, j, chunk=chunk: (chunk(b, j), 0)))
        side_shapes.append(jax.ShapeDtypeStruct((k_dim, n_dim), BF16))
    kern = functools.partial(_ffn_kernel, layer=layer, sub=sub, first=first, prompt=prompt,
                             n_side=len(side))
    res = pl.pallas_call(
        kern,
        grid=grid,
        in_specs=[
            x_spec, ada_spec,
            _const_spec((D_MODEL, D_FF), (0, 0)),
            _const_spec((D_MODEL, D_FF), (0, 0)),
            _const_spec((D_FF, D_MODEL), (0, 0)),
            _const_spec((DEPTH, 3, D_MODEL), (0, 0, 0)),
            _const_spec((DEPTH, 3, D_MODEL), (0, 0, 0)),
            _const_spec((1, D_MODEL), (0, 0)),
            _const_spec((1, D_MODEL), (0, 0)),
        ] + side_in_specs,
        out_specs=[x_spec] + side_out_specs,
        out_shape=[jax.ShapeDtypeStruct((rows, D_MODEL), F32)] + side_shapes,
        compiler_params=pltpu.CompilerParams(
            dimension_semantics=("arbitrary", "arbitrary"), vmem_limit_bytes=VMEM_LIMIT_BYTES),
        name=f"ffn{sub}_{'p' if prompt else 's'}{layer}",
    )(x2d, ada, wg, wu, wd, post_g, post_b, in_g, in_b, *[w for w, _ in side])
    return res[0], list(res[1:])


def _ffn_both_kernel(x_hbm, xs_hbm, adap_ref, adas_ref, wg_ref, wu_ref, wd_ref, lng_ref, lnb_ref,
                     ing_ref, inb_ref, *rest, layer, sub, first, n_side, specs):
    side_in_hbm, (o_hbm, os_hbm), side_out_hbm = rest[:n_side], rest[n_side:n_side + 2], rest[n_side + 2:]
    nt = SEQ // TM_FFN
    part = TM_FFN // ROW_PARTS

    def half_step(x, mod):
        if first:
            x = _ln(x, ing_ref[...], inb_ref[...])
        return _ffn_math(x, mod, wg_ref, wu_ref, wd_ref, lng_ref, lnb_ref, layer=layer, sub=sub)

    def prompt_body(x_ref, *tiles):
        side_in, o_ref, side_out = tiles[:n_side], tiles[n_side], tiles[n_side + 1:]
        b = lax.div(pl.program_id(0), nt)
        mod = lambda k: adap_ref[k % 3, pl.ds(b, 1), :]
        for i in range(ROW_PARTS):
            rows = slice(i * part, (i + 1) * part)
            o_ref[rows, :] = half_step(x_ref[rows, :], mod)
        for src, dst in zip(side_in, side_out):
            dst[...] = src[...].astype(BF16)

    def sample_body(x_ref, o_ref):
        o_ref[...] = half_step(x_ref[...], lambda k: adas_ref[k % 3])

    in_specs, out_specs, s_spec = specs
    pltpu.emit_pipeline(prompt_body, grid=(BATCH * nt,), in_specs=in_specs, out_specs=out_specs)(
        x_hbm, *side_in_hbm, o_hbm, *side_out_hbm)
    pltpu.emit_pipeline(sample_body, grid=(xs_hbm.shape[0] // s_spec.block_shape[0],),
                        in_specs=[s_spec], out_specs=[s_spec])(xs_hbm, os_hbm)


def _ffn_both_call(x2d, xs2d, ada, wg, wu, wd, post_g, post_b, in_g, in_b, *, layer, sub, first, side=()):
    n_steps = BATCH * (SEQ // TM_FFN)
    x_spec = pl.BlockSpec((TM_FFN, D_MODEL), lambda s: (s, 0))
    s_spec = pl.BlockSpec((TM_FFN // ROW_PARTS, D_MODEL), lambda s: (s, 0))
    k0 = 0 if sub == 0 else 6
    side_in_specs, side_out_specs, side_shapes = [], [], []
    for w, wl in side:
        k_dim, n_dim = w.shape[1:]
        steps = n_steps
        while (k_dim // steps) % BF16_SUBLANES:
            steps //= 2
        per = n_steps // steps
        side_in_specs.append(pl.BlockSpec((None, k_dim // steps, n_dim),
                                          lambda s, wl=wl, per=per: (wl, s // per, 0)))
        side_out_specs.append(pl.BlockSpec((k_dim // steps, n_dim), lambda s, per=per: (s // per, 0)))
        side_shapes.append(jax.ShapeDtypeStruct((k_dim, n_dim), BF16))
    in_hbm = pl.BlockSpec(memory_space=pl.ANY)
    in_vmem = pl.BlockSpec(memory_space=pltpu.VMEM)
    kern = functools.partial(_ffn_both_kernel, layer=layer, sub=sub, first=first, n_side=len(side),
                             specs=([x_spec] + side_in_specs, [x_spec] + side_out_specs, s_spec))
    res = pl.pallas_call(
        kern,
        in_specs=[in_hbm, in_hbm] + [in_vmem] * 9 + [in_hbm] * len(side),
        out_specs=[in_hbm] * (2 + len(side)),
        out_shape=[jax.ShapeDtypeStruct(x2d.shape, F32), jax.ShapeDtypeStruct(xs2d.shape, F32)] + side_shapes,
        compiler_params=pltpu.CompilerParams(vmem_limit_bytes=VMEM_LIMIT_BYTES),
        name=f"ffn{sub}_{layer}",
    )(x2d, xs2d, ada[layer, k0:k0 + 3, DEC_BATCH:], ada[layer, k0:k0 + 3, :DEC_BATCH], wg, wu, wd,
      post_g, post_b, in_g, in_b, *[w for w, _ in side])
    return res[0], res[1], list(res[2:])


def _pool_lane_consts():
    lane = lax.broadcasted_iota(jnp.int32, (1, G), 1)
    grp = lane // POOL_GDIM
    win = jnp.where(grp == 0, float(POOL_WINDOWS[0]),
                    jnp.where(grp == 1, float(POOL_WINDOWS[1]),
                              jnp.where(grp == 2, float(POOL_WINDOWS[2]), float(POOL_WINDOWS[3]))))
    return grp, win


def _pool_select(grp, s2, s4, s8, s16):
    return jnp.where(grp == 0, s2, jnp.where(grp == 1, s4, jnp.where(grp == 2, s8, s16)))


def _rms_group(y, g, eps=1e-6):
    return y * lax.rsqrt(jnp.mean(y * y, axis=-1, keepdims=True) + eps) * g


def _mixer_front(x, mod, win_ref):
    sh, sc = mod(3), mod(4)
    xm = (x * (1.0 + _rows(x, sc)) + _rows(x, sh)).astype(BF16)
    return _dot(xm, win_ref[...])


def _mixer_norm(ya, yb, yc, yd, onorm_ref, layer):
    parts = [ya, yb, yc, yd]
    normed = [_rms_group(p, onorm_ref[pl.ds(layer, 1), i * G:(i + 1) * G]).astype(BF16)
              for i, p in enumerate(parts)]
    return jnp.concatenate(normed, axis=-1)


def _mixer_back(x, mixn, mod, wout_ref, lng_ref, lnb_ref, layer):
    y = ALPHA * x + (1.0 + _rows(x, mod(5))) * _dot(mixn, wout_ref[...])
    return _ln(y, lng_ref[layer, pl.ds(1, 1), :], lnb_ref[layer, pl.ds(1, 1), :])


def _mixer_prompt_mid(proj, pos0, r0, last, sguw_ref, sgub_ref, sgug_ref, sgulb_ref,
                      poolw_ref, pools_ref, sconvw_ref, dconvw_ref, dconvb_ref, cvg_ref, cvb_ref,
                      onorm_ref, sguv_ref, poolst_ref, sconvst_ref, dconvst_ref,
                      pool_ext, sconv_ext, dconv_ext, *, layer):
    n = proj.shape[0]
    lrow = pl.ds(layer, 1)
    col = lambda i: proj[:, i * G:(i + 1) * G]

    u = _gelu(col(0))
    v = _ln(_gelu(col(1)), sgug_ref[lrow, :], sgulb_ref[lrow, :])
    if last:
        sguv_ref[...] = v[n - CHUNK:, :]
    r_i = lax.broadcasted_iota(jnp.int32, (CHUNK, CHUNK), 0)
    c_i = lax.broadcasted_iota(jnp.int32, (CHUNK, CHUNK), 1)
    wcat = jnp.concatenate([jnp.where(r_i >= c_i, sguw_ref[h], 0.0) for h in range(A_HEADS)],
                           axis=1).astype(BF16)
    head = lax.broadcasted_iota(jnp.int32, (1, G), 1) // HEAD_DIM
    bias = sgub_ref[...]
    mixed = []
    for c in range(n // CHUNK):
        vc = v[c * CHUNK:(c + 1) * CHUNK, :]
        rhs = jnp.concatenate([jnp.where(head == h, vc, 0.0) for h in range(A_HEADS)],
                              axis=0).astype(BF16)
        mixed.append(_dot(wcat, rhs) + bias)
    ya = u * jnp.concatenate(mixed, axis=0)

    xb = col(2)
    pool_ext[r0 + POOL_CARRY:r0 + POOL_CARRY + n, :] = xb
    e = pool_ext[r0:r0 + POOL_CARRY + n, :]
    s2 = e + pltpu.roll(e, 1, 0)
    s4 = s2 + pltpu.roll(s2, 2, 0)
    s8 = s4 + pltpu.roll(s4, 4, 0)
    s16 = s8 + pltpu.roll(s8, 8, 0)
    grp, win = _pool_lane_consts()
    ssel = _pool_select(grp, s2, s4, s8, s16)[POOL_CARRY:, :]
    pos = pos0 + lax.broadcasted_iota(jnp.int32, (n, 1), 0)
    cnt = jnp.minimum(win, (pos + 1).astype(F32))
    pooled = (ssel / cnt - xb).astype(BF16)
    yb = _dot(pooled, poolw_ref[...]) * pools_ref[lrow, :]
    if last:
        poolst_ref[...] = pool_ext[r0 + n + POOL_CARRY - POOL_BUF:r0 + n + POOL_CARRY, :]

    zc = col(4) * col(5)
    sconv_ext[r0 + SCONV_CARRY:r0 + SCONV_CARRY + n, :] = zc
    es = sconv_ext[r0:r0 + SCONV_CARRY + n, :]
    yconv = (sconvw_ref[layer, pl.ds(2, 1), :] * es
             + sconvw_ref[layer, pl.ds(1, 1), :] * pltpu.roll(es, 1, 0)
             + sconvw_ref[layer, pl.ds(0, 1), :] * pltpu.roll(es, 2, 0))
    yc = col(3) * yconv[SCONV_CARRY:, :]
    if last:
        sconvst_ref[...] = sconv_ext[r0 + n + SCONV_CARRY - (SCONV_W - 1):r0 + n + SCONV_CARRY, :]

    zd = col(6) * jax.nn.sigmoid(col(7))
    dconv_ext[r0 + DCONV_CARRY:r0 + DCONV_CARRY + n, :] = zd
    if last:
        dconvst_ref[...] = dconv_ext[r0 + n + DCONV_CARRY - (DCONV_W - 1):r0 + n + DCONV_CARRY, :]
    ed = dconv_ext[r0:r0 + DCONV_CARRY + n, :]
    acc = jnp.zeros((n, G), F32)
    for r in range(SUBLANES):
        er = ed if r == 0 else pltpu.roll(ed, r, 0)
        for q in range(DCONV_CARRY // SUBLANES):
            tap = SUBLANES * q + r
            if tap >= DCONV_W:
                continue
            lo = DCONV_CARRY - SUBLANES * q
            acc = acc + dconvw_ref[layer, pl.ds(DCONV_W - 1 - tap, 1), :] * er[lo:lo + n, :]
    hd = _ln(acc + dconvb_ref[lrow, :], cvg_ref[lrow, :], cvb_ref[lrow, :])
    yd = _silu(hd)

    return _mixer_norm(ya, yb, yc, yd, onorm_ref, layer)


def _mixer_prompt_kernel(x_ref, ada_ref, win_ref, wout_ref, sguw_ref, sgub_ref, sgug_ref, sgulb_ref,
                         poolw_ref, pools_ref, sconvw_ref, dconvw_ref, dconvb_ref, cvg_ref, cvb_ref,
                         onorm_ref, lng_ref, lnb_ref,
                         o_ref, sguv_ref, poolst_ref, sconvst_ref, dconvst_ref,
                         pool_ext, sconv_ext, dconv_ext, *, layer, sub_rows):
    tm = x_ref.shape[0]
    b = pl.program_id(0)
    j = pl.program_id(1)

    @pl.when(j == 0)
    def _():
        pool_ext[0:POOL_CARRY, :] = jnp.zeros((POOL_CARRY, G), F32)
        sconv_ext[0:SCONV_CARRY, :] = jnp.zeros((SCONV_CARRY, G), F32)
        dconv_ext[0:DCONV_CARRY, :] = jnp.zeros((DCONV_CARRY, G), F32)

    @pl.when(j > 0)
    def _():
        pool_ext[0:POOL_CARRY, :] = pool_ext[tm:tm + POOL_CARRY, :]
        sconv_ext[0:SCONV_CARRY, :] = sconv_ext[tm:tm + SCONV_CARRY, :]
        dconv_ext[0:DCONV_CARRY, :] = dconv_ext[tm:tm + DCONV_CARRY, :]

    mod = lambda k: ada_ref[k, pl.ds(b, 1), :]
    n_sub = tm // sub_rows
    rows = lambda h: slice(h * sub_rows, (h + 1) * sub_rows)
    front = lambda h: _mixer_front(x_ref[rows(h), :], mod, win_ref)
    proj = front(0)
    for h in range(n_sub):
        proj_next = front(h + 1) if h + 1 < n_sub else None
        mixn = _mixer_prompt_mid(
            proj, j * tm + h * sub_rows, h * sub_rows, h == n_sub - 1,
            sguw_ref, sgub_ref, sgug_ref, sgulb_ref, poolw_ref, pools_ref, sconvw_ref,
            dconvw_ref, dconvb_ref, cvg_ref, cvb_ref, onorm_ref,
            sguv_ref, poolst_ref, sconvst_ref, dconvst_ref, pool_ext, sconv_ext, dconv_ext, layer=layer)
        o_ref[rows(h), :] = _mixer_back(x_ref[rows(h), :], mixn, mod, wout_ref, lng_ref, lnb_ref, layer)
        proj = proj_next


def _mixer_prompt_call(x2d, ada, w_in, w_out, sgu_w, sgu_bias, small, *, layer):
    (sgu_ln_g, sgu_ln_b, pool_bd, pool_scale, sconv_w, dconv_w, dconv_b, conv_ln_g, conv_ln_b,
     out_norm_g, post_g, post_b) = small
    tm = TM_MIX
    nt = SEQ // tm
    x_spec = pl.BlockSpec((tm, D_MODEL), lambda b, j: (b * nt + j, 0))
    st_spec = lambda n: pl.BlockSpec((None, n, G), lambda b, j: (b, 0, 0))
    full2 = lambda a: _const_spec(a.shape, (0,) * a.ndim)
    return pl.pallas_call(
        functools.partial(_mixer_prompt_kernel, layer=layer, sub_rows=SUB_MIX),
        grid=(BATCH, nt),
        in_specs=[
            x_spec,
            _const_spec((None, N_ADA, BATCH, D_MODEL), (layer, 0, DEC_BATCH // BATCH, 0)),
            _const_spec((D_MODEL, IN_COLS), (0, 0)),
            _const_spec((D_MODEL, D_MODEL), (0, 0)),
            _const_spec((None, A_HEADS, CHUNK, CHUNK), (layer, 0, 0, 0)),
            _const_spec((None, CHUNK, G), (layer, 0, 0)),
            full2(sgu_ln_g), full2(sgu_ln_b),
            _const_spec((None, G, G), (layer, 0, 0)),
            full2(pool_scale), full2(sconv_w), full2(dconv_w), full2(dconv_b),
            full2(conv_ln_g), full2(conv_ln_b), full2(out_norm_g), full2(post_g), full2(post_b),
        ],
        out_specs=[x_spec, st_spec(CHUNK), st_spec(POOL_BUF), st_spec(SCONV_W - 1), st_spec(DCONV_W - 1)],
        out_shape=[
            jax.ShapeDtypeStruct((BATCH * SEQ, D_MODEL), F32),
            jax.ShapeDtypeStruct((BATCH, CHUNK, G), F32),
            jax.ShapeDtypeStruct((BATCH, POOL_BUF, G), F32),
            jax.ShapeDtypeStruct((BATCH, SCONV_W - 1, G), F32),
            jax.ShapeDtypeStruct((BATCH, DCONV_W - 1, G), F32),
        ],
        scratch_shapes=[
            pltpu.VMEM((tm + POOL_CARRY, G), F32),
            pltpu.VMEM((tm + SCONV_CARRY, G), F32),
            pltpu.VMEM((tm + DCONV_CARRY, G), F32),
        ],
        compiler_params=pltpu.CompilerParams(
            dimension_semantics=("arbitrary", "arbitrary"), vmem_limit_bytes=VMEM_LIMIT_BYTES),
        name=f"mixer_p{layer}",
    )(x2d, ada, w_in, w_out, sgu_w, sgu_bias, sgu_ln_g, sgu_ln_b, pool_bd, pool_scale, sconv_w,
      dconv_w, dconv_b, conv_ln_g, conv_ln_b, out_norm_g, post_g, post_b)


def _mixer_sample_kernel(x_ref, ada_ref, win_ref, wout_ref, sguw_ref, sgub_ref, sgug_ref, sgulb_ref,
                         poolw_ref, pools_ref, sconvw_ref, dconvw_ref, dconvb_ref, cvg_ref, cvb_ref,
                         onorm_ref, lng_ref, lnb_ref, poolin_ref, sconvin_ref, dconvin_ref,
                         o_ref, sguv_ref, poolst_ref, sconvst_ref, dconvst_ref, *, layer):
    nb = DEC_BATCH
    lrow = pl.ds(layer, 1)
    x = x_ref[...]
    mod = lambda k: ada_ref[k]
    proj = _mixer_front(x, mod, win_ref)
    col = lambda i: proj[:, i * G:(i + 1) * G]
    slab = lambda a, t: a[t * nb:(t + 1) * nb, :]

    u = _gelu(col(0))
    v = _ln(_gelu(col(1)), sgug_ref[lrow, :], sgulb_ref[lrow, :])
    for t in range(DEC_SEQ):
        sguv_ref[t] = slab(v, t)
    mixed = []
    for t in range(DEC_SEQ):
        m = jnp.zeros((nb, G), F32) + sgub_ref[pl.ds(t, 1), :]
        for s in range(t + 1):
            m = m + sguw_ref[t, pl.ds(s, 1), :] * slab(v, s)
        mixed.append(m)
    ya = u * jnp.concatenate(mixed, axis=0)

    xb = col(2)
    e = [poolin_ref[i] for i in range(POOL_BUF)] + [slab(xb, t) for t in range(DEC_SEQ)]
    n_e = len(e)
    s2 = [e[i] + e[i - 1] if i >= 1 else None for i in range(n_e)]
    s4 = [s2[i] + s2[i - 2] if i >= 3 else None for i in range(n_e)]
    s8 = [s4[i] + s4[i - 4] if i >= 7 else None for i in range(n_e)]
    s16 = [s8[i] + s8[i - 8] if i >= 15 else None for i in range(n_e)]
    grp, win = _pool_lane_consts()
    cnt = jnp.minimum(win, float(PAST_LEN + 1))
    sel = jnp.concatenate([_pool_select(grp, s2[i], s4[i], s8[i], s16[i])
                           for i in range(POOL_BUF, n_e)], axis=0)
    pooled = (sel / cnt - xb).astype(BF16)
    yb = _dot(pooled, poolw_ref[...]) * pools_ref[lrow, :]
    for i in range(POOL_BUF):
        poolst_ref[i] = e[n_e - POOL_BUF + i]

    zc = col(4) * col(5)
    es = [sconvin_ref[i] for i in range(SCONV_W - 1)] + [slab(zc, t) for t in range(DEC_SEQ)]
    yconv = jnp.concatenate(
        [sum(sconvw_ref[layer, pl.ds(k, 1), :] * es[t + k] for k in range(SCONV_W))
         for t in range(DEC_SEQ)], axis=0)
    yc = col(3) * yconv
    for i in range(SCONV_W - 1):
        sconvst_ref[i] = es[len(es) - (SCONV_W - 1) + i]

    zd = col(6) * jax.nn.sigmoid(col(7))
    for i in range(DCONV_W - 1 - DEC_SEQ):
        dconvst_ref[i] = dconvin_ref[i + DEC_SEQ]
    for t in range(DEC_SEQ):
        dconvst_ref[DCONV_W - 1 - DEC_SEQ + t] = slab(zd, t)
    conv = []
    for t in range(DEC_SEQ):
        a = jnp.zeros((nb, G), F32)
        for k in range(DCONV_W):
            i = t + k
            src = dconvin_ref[i] if i < DCONV_W - 1 else slab(zd, i - (DCONV_W - 1))
            a = a + dconvw_ref[layer, pl.ds(k, 1), :] * src
        conv.append(a)
    hd = _ln(jnp.concatenate(conv, axis=0) + dconvb_ref[lrow, :], cvg_ref[lrow, :], cvb_ref[lrow, :])
    yd = _silu(hd)

    mixn = _mixer_norm(ya, yb, yc, yd, onorm_ref, layer)
    o_ref[...] = _mixer_back(x, mixn, mod, wout_ref, lng_ref, lnb_ref, layer)


def _mixer_sample_call(x2d, ada, w_in, w_out, sgu_wsm, sgu_bias, small, pool_in, sconv_in, dconv_in,
                       *, layer):
    (sgu_ln_g, sgu_ln_b, pool_bd, pool_scale, sconv_w, dconv_w, dconv_b, conv_ln_g, conv_ln_b,
     out_norm_g, post_g, post_b) = small
    rows = DEC_SEQ * DEC_BATCH
    full2 = lambda a: _const_spec(a.shape, (0,) * a.ndim)
    st = lambda n: _const_spec((None, n, DEC_BATCH, G), (layer, 0, 0, 0))
    st_out = lambda n: pl.BlockSpec((n, DEC_BATCH, G), lambda i: (0, 0, 0))
    return pl.pallas_call(
        functools.partial(_mixer_sample_kernel, layer=layer),
        grid=(1,),
        in_specs=[
            _const_spec((rows, D_MODEL), (0, 0)),
            _const_spec((None, N_ADA, DEC_BATCH, D_MODEL), (layer, 0, 0, 0)),
            _const_spec((D_MODEL, IN_COLS), (0, 0)),
            _const_spec((D_MODEL, D_MODEL), (0, 0)),
            _const_spec((None, DEC_SEQ, DEC_SEQ, G), (layer, 0, 0, 0)),
            _const_spec((None, DEC_SEQ, G), (layer, 0, 0)),
            full2(sgu_ln_g), full2(sgu_ln_b),
            _const_spec((None, G, G), (layer, 0, 0)),
            full2(pool_scale), full2(sconv_w), full2(dconv_w), full2(dconv_b),
            full2(conv_ln_g), full2(conv_ln_b), full2(out_norm_g), full2(post_g), full2(post_b),
            st(POOL_BUF), st(SCONV_W - 1), st(DCONV_W - 1),
        ],
        out_specs=[pl.BlockSpec((rows, D_MODEL), lambda i: (0, 0)),
                   st_out(DEC_SEQ), st_out(POOL_BUF), st_out(SCONV_W - 1), st_out(DCONV_W - 1)],
        out_shape=[
            jax.ShapeDtypeStruct((rows, D_MODEL), F32),
            jax.ShapeDtypeStruct((DEC_SEQ, DEC_BATCH, G), F32),
            jax.ShapeDtypeStruct((POOL_BUF, DEC_BATCH, G), F32),
            jax.ShapeDtypeStruct((SCONV_W - 1, DEC_BATCH, G), F32),
            jax.ShapeDtypeStruct((DCONV_W - 1, DEC_BATCH, G), F32),
        ],
        compiler_params=pltpu.CompilerParams(
            dimension_semantics=("arbitrary",), vmem_limit_bytes=VMEM_LIMIT_BYTES),
        name=f"mixer_s{layer}",
    )(x2d, ada, w_in, w_out, sgu_wsm, sgu_bias, sgu_ln_g, sgu_ln_b, pool_bd, pool_scale, sconv_w,
      dconv_w, dconv_b, conv_ln_g, conv_ln_b, out_norm_g, post_g, post_b, pool_in, sconv_in, dconv_in)


def kernel(x_prompt, x_sample, state_pool, state_sconv, state_dconv, c_prompt, c_sample, ln_in_g, ln_in_b, w_ada, b_ada, ffn1_w_gate, ffn1_w_up, ffn1_w_down, w_in, sgu_ln_g, sgu_ln_b, sgu_w, sgu_b, pool_w, pool_scale, sconv_w, dconv_w, dconv_b, conv_ln_g, conv_ln_b, out_norm_g, w_out, ffn2_w_gate, ffn2_w_up, ffn2_w_down, post_ln_g, post_ln_b):
    eye = jnp.eye(len(POOL_WINDOWS), dtype=F32)
    pool_bd = jnp.einsum('lgcd,gh->lgchd', pool_w, eye).reshape(DEPTH, G, G).astype(BF16)
    sgu_bias = jnp.repeat(jnp.transpose(sgu_b, (0, 2, 1)), HEAD_DIM, axis=-1)
    sgu_wsm = jnp.repeat(jnp.transpose(sgu_w[:, :, :DEC_SEQ, :DEC_SEQ], (0, 2, 3, 1)),
                         HEAD_DIM, axis=-1)
    in_g, in_b = ln_in_g.reshape(1, D_MODEL), ln_in_b.reshape(1, D_MODEL)
    small = (sgu_ln_g, sgu_ln_b, pool_bd, pool_scale, sconv_w, dconv_w, dconv_b, conv_ln_g,
             conv_ln_b, out_norm_g, post_ln_g, post_ln_b)

    ada = _ada_call(jnp.concatenate([c_sample, c_prompt], axis=0), w_ada, b_ada)

    xs = jnp.transpose(x_sample, (1, 0, 2)).reshape(DEC_SEQ * DEC_BATCH, D_MODEL)
    pool_in = jnp.transpose(state_pool, (0, 2, 1, 3))
    sconv_in = jnp.transpose(state_sconv, (0, 2, 1, 3))
    dconv_in = jnp.transpose(state_dconv, (0, 2, 1, 3))
    xp = x_prompt.reshape(BATCH * SEQ, D_MODEL)

    ffn1 = [w[0].astype(BF16) for w in (ffn1_w_gate, ffn1_w_up, ffn1_w_down)]
    ffn = functools.partial(_ffn_both_call, post_g=post_ln_g, post_b=post_ln_b, in_g=in_g, in_b=in_b)
    outs_p, outs_s = [], []
    for l in range(DEPTH):
        last = l == DEPTH - 1
        side = [(w_in, l), (w_out, l), (ffn2_w_gate, l), (ffn2_w_up, l), (ffn2_w_down, l)]
        xp, xs, (w_in_b, w_out_b, *ffn2) = ffn(xp, xs, ada, *ffn1, layer=l, sub=0, first=(l == 0), side=side)
        xp, *st_p = _mixer_prompt_call(xp, ada, w_in_b, w_out_b, sgu_w, sgu_bias, small, layer=l)
        xs, *st_s = _mixer_sample_call(xs, ada, w_in_b, w_out_b, sgu_wsm, sgu_bias, small,
                                       pool_in, sconv_in, dconv_in, layer=l)
        side = [] if last else [(ffn1_w_gate, l + 1), (ffn1_w_up, l + 1), (ffn1_w_down, l + 1)]
        xp, xs, ffn1 = ffn(xp, xs, ada, *ffn2, layer=l, sub=2, first=False, side=side)
        outs_p.append(st_p)
        outs_s.append(st_s)

    y_prompt = xp.reshape(BATCH, SEQ, D_MODEL)
    y_sample = jnp.transpose(xs.reshape(DEC_SEQ, DEC_BATCH, D_MODEL), (1, 0, 2))
    stack_p = lambda i: jnp.stack([o[i] for o in outs_p], axis=0)
    stack_s = lambda i: jnp.transpose(jnp.stack([o[i] for o in outs_s], axis=0), (0, 2, 1, 3))
    return (y_prompt, y_sample, stack_p(0), stack_s(0), stack_p(1), stack_s(1),
            stack_p(2), stack_s(2), stack_p(3), stack_s(3))
```

```python
import functools

import jax
import jax.numpy as jnp
from jax import lax
from jax.experimental import pallas as pl
from jax.experimental.pallas import tpu as pltpu

D_MODEL = 1024
BATCH = 8
SEQ = 2048
DEPTH = 2
DEC_BATCH = 128
DEC_SEQ = 8
PAST_LEN = 16384
G = 256
HEAD_DIM = 64
A_HEADS = G // HEAD_DIM
CHUNK = 128
POOL_WINDOWS = (2, 4, 8, 16)
POOL_GDIM = G // len(POOL_WINDOWS)
POOL_BUF = 15
SCONV_W = 3
DCONV_W = 31
D_FF = 2816
N_ADA = 9
ALPHA = (2 * DEPTH) ** 0.25
IN_COLS = 8 * G
N_SEQ_ALL = DEC_BATCH + BATCH
ADA_PER_STEP = 3

SUBLANES = 8
BF16_SUBLANES = 16
POOL_CARRY = 16
SCONV_CARRY = 16
DCONV_CARRY = 32
VMEM_LIMIT_BYTES = 56 * 1024 * 1024

TM_FFN = 1024
ROW_PARTS = 2
TM_MIX = 1024
SUB_MIX = 512

BF16 = jnp.bfloat16
F32 = jnp.float32


def _ln(x, g, b, eps=1e-5):
    mu = jnp.mean(x, axis=-1, keepdims=True)
    xc = x - mu
    var = jnp.mean(xc * xc, axis=-1, keepdims=True)
    return xc * lax.rsqrt(var + eps) * g + b


def _gelu(x):
    return 0.5 * x * (1.0 + lax.erf(x * 0.7071067811865476))


def _silu(x):
    return x * jax.nn.sigmoid(x)


def _rows(x, m):
    if m.shape[0] == 1 or m.shape[0] == x.shape[0]:
        return m
    reps = x.shape[0] // m.shape[0]
    return jnp.broadcast_to(m[None], (reps,) + m.shape).reshape(x.shape)


def _dot(a, b):
    return jnp.dot(a, b, preferred_element_type=F32)


def _const_spec(shape, index):
    return pl.BlockSpec(shape, lambda *_: index, pipeline_mode=pl.Buffered(1))


def _ada_kernel(c_ref, w_ref, b_ref, o_ref):
    s = _silu(c_ref[...]).astype(BF16)
    y = _dot(s, w_ref[...].astype(BF16))
    for k in range(ADA_PER_STEP):
        cols = slice(k * D_MODEL, (k + 1) * D_MODEL)
        o_ref[k] = y[:, cols] + b_ref[pl.ds(pl.program_id(0), 1), cols]


def _ada_call(c_all, w_ada, b_ada):
    return pl.pallas_call(
        _ada_kernel,
        grid=(DEPTH, N_ADA // ADA_PER_STEP),
        in_specs=[
            pl.BlockSpec((N_SEQ_ALL, D_MODEL), lambda l, k: (0, 0)),
            pl.BlockSpec((None, D_MODEL, ADA_PER_STEP * D_MODEL), lambda l, k: (l, 0, k)),
            pl.BlockSpec((DEPTH, ADA_PER_STEP * D_MODEL), lambda l, k: (0, k)),
        ],
        out_specs=pl.BlockSpec((None, ADA_PER_STEP, N_SEQ_ALL, D_MODEL), lambda l, k: (l, k, 0, 0)),
        out_shape=jax.ShapeDtypeStruct((DEPTH, N_ADA, N_SEQ_ALL, D_MODEL), F32),
        compiler_params=pltpu.CompilerParams(
            dimension_semantics=("arbitrary", "arbitrary"), vmem_limit_bytes=VMEM_LIMIT_BYTES),
        name="ada",
    )(c_all, w_ada, b_ada)


def _ffn_math(x, mod, wg_ref, wu_ref, wd_ref, lng_ref, lnb_ref, *, layer, sub):
    k0 = 0 if sub == 0 else 6
    sh, sc, gt = mod(k0), mod(k0 + 1), mod(k0 + 2)
    xm = (x * (1.0 + _rows(x, sc)) + _rows(x, sh)).astype(BF16)
    hg = _dot(xm, wg_ref[...])
    hu = _dot(xm, wu_ref[...])
    h = (_silu(hg) * hu).astype(BF16)
    gain = 0.5 * (1.0 + _rows(x, gt))
    half = x.shape[0] // 2
    outs = []
    for r in (slice(0, half), slice(half, 2 * half)):
        y = ALPHA * x[r] + (gain if gain.shape[0] == 1 else gain[r]) * _dot(h[r], wd_ref[...])
        outs.append(_ln(y, lng_ref[layer, pl.ds(sub, 1), :], lnb_ref[layer, pl.ds(sub, 1), :]))
    return jnp.concatenate(outs, axis=0)


def _ffn_both_kernel(x_hbm, xs_hbm, adap_ref, adas_ref, wg_ref, wu_ref, wd_ref, lng_ref, lnb_ref,
                     ing_ref, inb_ref, *rest, layer, sub, first, n_side, specs):
    side_in_hbm, (o_hbm, os_hbm), side_out_hbm = rest[:n_side], rest[n_side:n_side + 2], rest[n_side + 2:]
    nt = SEQ // TM_FFN
    part = TM_FFN // ROW_PARTS

    def half_step(x, mod):
        if first:
            x = _ln(x, ing_ref[...], inb_ref[...])
        return _ffn_math(x, mod, wg_ref, wu_ref, wd_ref, lng_ref, lnb_ref, layer=layer, sub=sub)

    def prompt_body(x_ref, *tiles):
        side_in, o_ref, side_out = tiles[:n_side], tiles[n_side], tiles[n_side + 1:]
        b = lax.div(pl.program_id(0), nt)
        mod = lambda k: adap_ref[k % 3, pl.ds(b, 1), :]
        for i in range(ROW_PARTS):
            rows = slice(i * part, (i + 1) * part)
            o_ref[rows, :] = half_step(x_ref[rows, :], mod)
        for src, dst in zip(side_in, side_out):
            dst[...] = src[...].astype(BF16)

    def sample_body(x_ref, o_ref):
        o_ref[...] = half_step(x_ref[...], lambda k: adas_ref[k % 3])

    in_specs, out_specs, s_spec = specs
    pltpu.emit_pipeline(prompt_body, grid=(BATCH * nt,), in_specs=in_specs, out_specs=out_specs)(
        x_hbm, *side_in_hbm, o_hbm, *side_out_hbm)
    pltpu.emit_pipeline(sample_body, grid=(xs_hbm.shape[0] // s_spec.block_shape[0],),
                        in_specs=[s_spec], out_specs=[s_spec])(xs_hbm, os_hbm)


def _ffn_both_call(x2d, xs2d, ada, wg, wu, wd, post_g, post_b, in_g, in_b, *, layer, sub, first, side=()):
    n_steps = BATCH * (SEQ // TM_FFN)
    x_spec = pl.BlockSpec((TM_FFN, D_MODEL), lambda s: (s, 0))
    s_spec = pl.BlockSpec((TM_FFN // ROW_PARTS, D_MODEL), lambda s: (s, 0))
    k0 = 0 if sub == 0 else 6
    side_in_specs, side_out_specs, side_shapes = [], [], []
    for w, wl in side:
        k_dim, n_dim = w.shape[1:]
        steps = n_steps
        while (k_dim // steps) % BF16_SUBLANES:
            steps //= 2
        per = n_steps // steps
        side_in_specs.append(pl.BlockSpec((None, k_dim // steps, n_dim),
                                          lambda s, wl=wl, per=per: (wl, s // per, 0)))
        side_out_specs.append(pl.BlockSpec((k_dim // steps, n_dim), lambda s, per=per: (s // per, 0)))
        side_shapes.append(jax.ShapeDtypeStruct((k_dim, n_dim), BF16))
    in_hbm = pl.BlockSpec(memory_space=pl.ANY)
    kern = functools.partial(_ffn_both_kernel, layer=layer, sub=sub, first=first, n_side=len(side),
                             specs=([x_spec] + side_in_specs, [x_spec] + side_out_specs, s_spec))
    res = pl.pallas_call(
        kern,
        grid=(1,),
        in_specs=[in_hbm, in_hbm,
                  _const_spec((None, 3, BATCH, D_MODEL), (layer, k0 // 3, DEC_BATCH // BATCH, 0)),
                  _const_spec((None, 3, DEC_BATCH, D_MODEL), (layer, k0 // 3, 0, 0))]
                 + [_const_spec(a.shape, (0,) * a.ndim) for a in (wg, wu, wd, post_g, post_b, in_g, in_b)]
                 + [in_hbm] * len(side),
        out_specs=[in_hbm] * (2 + len(side)),
        out_shape=[jax.ShapeDtypeStruct(x2d.shape, F32), jax.ShapeDtypeStruct(xs2d.shape, F32)] + side_shapes,
        compiler_params=pltpu.CompilerParams(vmem_limit_bytes=VMEM_LIMIT_BYTES),
        name=f"ffn{sub}_{layer}",
    )(x2d, xs2d, ada, ada, wg, wu, wd, post_g, post_b, in_g, in_b, *[w for w, _ in side])
    return res[0], res[1], list(res[2:])


def _pool_lane_consts():
    lane = lax.broadcasted_iota(jnp.int32, (1, G), 1)
    grp = lane // POOL_GDIM
    win = jnp.where(grp == 0, float(POOL_WINDOWS[0]),
                    jnp.where(grp == 1, float(POOL_WINDOWS[1]),
                              jnp.where(grp == 2, float(POOL_WINDOWS[2]), float(POOL_WINDOWS[3]))))
    return grp, win


def _pool_select(grp, s2, s4, s8, s16):
    return jnp.where(grp == 0, s2, jnp.where(grp == 1, s4, jnp.where(grp == 2, s8, s16)))


def _rms_group(y, g, eps=1e-6):
    return y * lax.rsqrt(jnp.mean(y * y, axis=-1, keepdims=True) + eps) * g


def _mixer_front(x, mod, win_ref):
    sh, sc = mod(3), mod(4)
    xm = (x * (1.0 + _rows(x, sc)) + _rows(x, sh)).astype(BF16)
    return _dot(xm, win_ref[...])


def _mixer_norm(ya, yb, yc, yd, onorm_ref, layer):
    parts = [ya, yb, yc, yd]
    normed = [_rms_group(p, onorm_ref[pl.ds(layer, 1), i * G:(i + 1) * G]).astype(BF16)
              for i, p in enumerate(parts)]
    return jnp.concatenate(normed, axis=-1)


def _mixer_back(x, mixn, mod, wout_ref, lng_ref, lnb_ref, layer):
    y = ALPHA * x + (1.0 + _rows(x, mod(5))) * _dot(mixn, wout_ref[...])
    return _ln(y, lng_ref[layer, pl.ds(1, 1), :], lnb_ref[layer, pl.ds(1, 1), :])


def _mixer_prompt_mid(proj, pos0, r0, last, sguw_ref, sgub_ref, sgug_ref, sgulb_ref,
                      poolw_ref, pools_ref, sconvw_ref, dconvw_ref, dconvb_ref, cvg_ref, cvb_ref,
                      onorm_ref, sguv_ref, poolst_ref, sconvst_ref, dconvst_ref,
                      pool_ext, sconv_ext, dconv_ext, *, layer):
    n = proj.shape[0]
    lrow = pl.ds(layer, 1)
    col = lambda i: proj[:, i * G:(i + 1) * G]

    u = _gelu(col(0))
    v = _ln(_gelu(col(1)), sgug_ref[lrow, :], sgulb_ref[lrow, :])
    if last:
        sguv_ref[...] = v[n - CHUNK:, :]
    r_i = lax.broadcasted_iota(jnp.int32, (CHUNK, CHUNK), 0)
    c_i = lax.broadcasted_iota(jnp.int32, (CHUNK, CHUNK), 1)
    wcat = jnp.concatenate([jnp.where(r_i >= c_i, sguw_ref[h], 0.0) for h in range(A_HEADS)],
                           axis=1).astype(BF16)
    head = lax.broadcasted_iota(jnp.int32, (1, G), 1) // HEAD_DIM
    bias = sgub_ref[...]
    mixed = []
    for c in range(n // CHUNK):
        vc = v[c * CHUNK:(c + 1) * CHUNK, :]
        rhs = jnp.concatenate([jnp.where(head == h, vc, 0.0) for h in range(A_HEADS)],
                              axis=0).astype(BF16)
        mixed.append(_dot(wcat, rhs) + bias)
    ya = u * jnp.concatenate(mixed, axis=0)

    xb = col(2)
    pool_ext[r0 + POOL_CARRY:r0 + POOL_CARRY + n, :] = xb
    e = pool_ext[r0:r0 + POOL_CARRY + n, :]
    s2 = e + pltpu.roll(e, 1, 0)
    s4 = s2 + pltpu.roll(s2, 2, 0)
    s8 = s4 + pltpu.roll(s4, 4, 0)
    s16 = s8 + pltpu.roll(s8, 8, 0)
    grp, win = _pool_lane_consts()
    ssel = _pool_select(grp, s2, s4, s8, s16)[POOL_CARRY:, :]
    pos = pos0 + lax.broadcasted_iota(jnp.int32, (n, 1), 0)
    cnt = jnp.minimum(win, (pos + 1).astype(F32))
    pooled = (ssel / cnt - xb).astype(BF16)
    yb = _dot(pooled, poolw_ref[...]) * pools_ref[lrow, :]
    if last:
        poolst_ref[...] = pool_ext[r0 + n + POOL_CARRY - POOL_BUF:r0 + n + POOL_CARRY, :]

    zc = col(4) * col(5)
    sconv_ext[r0 + SCONV_CARRY:r0 + SCONV_CARRY + n, :] = zc
    es = sconv_ext[r0:r0 + SCONV_CARRY + n, :]
    yconv = (sconvw_ref[layer, pl.ds(2, 1), :] * es
             + sconvw_ref[layer, pl.ds(1, 1), :] * pltpu.roll(es, 1, 0)
             + sconvw_ref[layer, pl.ds(0, 1), :] * pltpu.roll(es, 2, 0))
    yc = col(3) * yconv[SCONV_CARRY:, :]
    if last:
        sconvst_ref[...] = sconv_ext[r0 + n + SCONV_CARRY - (SCONV_W - 1):r0 + n + SCONV_CARRY, :]

    zd = col(6) * jax.nn.sigmoid(col(7))
    dconv_ext[r0 + DCONV_CARRY:r0 + DCONV_CARRY + n, :] = zd
    if last:
        dconvst_ref[...] = dconv_ext[r0 + n + DCONV_CARRY - (DCONV_W - 1):r0 + n + DCONV_CARRY, :]
    ed = dconv_ext[r0:r0 + DCONV_CARRY + n, :]
    acc = jnp.zeros((n, G), F32)
    for r in range(SUBLANES):
        er = ed if r == 0 else pltpu.roll(ed, r, 0)
        for q in range(DCONV_CARRY // SUBLANES):
            tap = SUBLANES * q + r
            if tap >= DCONV_W:
                continue
            lo = DCONV_CARRY - SUBLANES * q
            acc = acc + dconvw_ref[layer, pl.ds(DCONV_W - 1 - tap, 1), :] * er[lo:lo + n, :]
    hd = _ln(acc + dconvb_ref[lrow, :], cvg_ref[lrow, :], cvb_ref[lrow, :])
    yd = _silu(hd)

    return _mixer_norm(ya, yb, yc, yd, onorm_ref, layer)


def _mixer_prompt_kernel(x_ref, ada_ref, win_ref, wout_ref, sguw_ref, sgub_ref, sgug_ref, sgulb_ref,
                         poolw_ref, pools_ref, sconvw_ref, dconvw_ref, dconvb_ref, cvg_ref, cvb_ref,
                         onorm_ref, lng_ref, lnb_ref,
                         o_ref, sguv_ref, poolst_ref, sconvst_ref, dconvst_ref,
                         pool_ext, sconv_ext, dconv_ext, *, layer, sub_rows):
    tm = x_ref.shape[0]
    b = pl.program_id(0)
    j = pl.program_id(1)

    @pl.when(j == 0)
    def _():
        pool_ext[0:POOL_CARRY, :] = jnp.zeros((POOL_CARRY, G), F32)
        sconv_ext[0:SCONV_CARRY, :] = jnp.zeros((SCONV_CARRY, G), F32)
        dconv_ext[0:DCONV_CARRY, :] = jnp.zeros((DCONV_CARRY, G), F32)

    @pl.when(j > 0)
    def _():
        pool_ext[0:POOL_CARRY, :] = pool_ext[tm:tm + POOL_CARRY, :]
        sconv_ext[0:SCONV_CARRY, :] = sconv_ext[tm:tm + SCONV_CARRY, :]
        dconv_ext[0:DCONV_CARRY, :] = dconv_ext[tm:tm + DCONV_CARRY, :]

    mod = lambda k: ada_ref[k, pl.ds(b, 1), :]
    n_sub = tm // sub_rows
    rows = lambda h: slice(h * sub_rows, (h + 1) * sub_rows)
    front = lambda h: _mixer_front(x_ref[rows(h), :], mod, win_ref)
    proj = front(0)
    for h in range(n_sub):
        proj_next = front(h + 1) if h + 1 < n_sub else None
        mixn = _mixer_prompt_mid(
            proj, j * tm + h * sub_rows, h * sub_rows, h == n_sub - 1,
            sguw_ref, sgub_ref, sgug_ref, sgulb_ref, poolw_ref, pools_ref, sconvw_ref,
            dconvw_ref, dconvb_ref, cvg_ref, cvb_ref, onorm_ref,
            sguv_ref, poolst_ref, sconvst_ref, dconvst_ref, pool_ext, sconv_ext, dconv_ext, layer=layer)
        o_ref[rows(h), :] = _mixer_back(x_ref[rows(h), :], mixn, mod, wout_ref, lng_ref, lnb_ref, layer)
        proj = proj_next


def _mixer_prompt_call(x2d, ada, w_in, w_out, sgu_w, sgu_bias, small, *, layer):
    (sgu_ln_g, sgu_ln_b, pool_bd, pool_scale, sconv_w, dconv_w, dconv_b, conv_ln_g, conv_ln_b,
     out_norm_g, post_g, post_b) = small
    tm = TM_MIX
    nt = SEQ // tm
    x_spec = pl.BlockSpec((tm, D_MODEL), lambda b, j: (b * nt + j, 0))
    st_spec = lambda n: pl.BlockSpec((None, n, G), lambda b, j: (b, 0, 0))
    full2 = lambda a: _const_spec(a.shape, (0,) * a.ndim)
    return pl.pallas_call(
        functools.partial(_mixer_prompt_kernel, layer=layer, sub_rows=SUB_MIX),
        grid=(BATCH, nt),
        in_specs=[
            x_spec,
            _const_spec((None, N_ADA, BATCH, D_MODEL), (layer, 0, DEC_BATCH // BATCH, 0)),
            _const_spec((D_MODEL, IN_COLS), (0, 0)),
            _const_spec((D_MODEL, D_MODEL), (0, 0)),
            _const_spec((None, A_HEADS, CHUNK, CHUNK), (layer, 0, 0, 0)),
            _const_spec((None, CHUNK, G), (layer, 0, 0)),
            full2(sgu_ln_g), full2(sgu_ln_b),
            _const_spec((None, G, G), (layer, 0, 0)),
            full2(pool_scale), full2(sconv_w), full2(dconv_w), full2(dconv_b),
            full2(conv_ln_g), full2(conv_ln_b), full2(out_norm_g), full2(post_g), full2(post_b),
        ],
        out_specs=[x_spec, st_spec(CHUNK), st_spec(POOL_BUF), st_spec(SCONV_W - 1), st_spec(DCONV_W - 1)],
        out_shape=[
            jax.ShapeDtypeStruct((BATCH * SEQ, D_MODEL), F32),
            jax.ShapeDtypeStruct((BATCH, CHUNK, G), F32),
            jax.ShapeDtypeStruct((BATCH, POOL_BUF, G), F32),
            jax.ShapeDtypeStruct((BATCH, SCONV_W - 1, G), F32),
            jax.ShapeDtypeStruct((BATCH, DCONV_W - 1, G), F32),
        ],
        scratch_shapes=[
            pltpu.VMEM((tm + POOL_CARRY, G), F32),
            pltpu.VMEM((tm + SCONV_CARRY, G), F32),
            pltpu.VMEM((tm + DCONV_CARRY, G), F32),
        ],
        compiler_params=pltpu.CompilerParams(
            dimension_semantics=("arbitrary", "arbitrary"), vmem_limit_bytes=VMEM_LIMIT_BYTES),
        name=f"mixer_p{layer}",
    )(x2d, ada, w_in, w_out, sgu_w, sgu_bias, sgu_ln_g, sgu_ln_b, pool_bd, pool_scale, sconv_w,
      dconv_w, dconv_b, conv_ln_g, conv_ln_b, out_norm_g, post_g, post_b)


def _mixer_sample_kernel(x_ref, ada_ref, win_ref, wout_ref, sguw_ref, sgub_ref, sgug_ref, sgulb_ref,
                         poolw_ref, pools_ref, sconvw_ref, dconvw_ref, dconvb_ref, cvg_ref, cvb_ref,
                         onorm_ref, lng_ref, lnb_ref, poolin_ref, sconvin_ref, dconvin_ref,
                         o_ref, sguv_ref, poolst_ref, sconvst_ref, dconvst_ref, *, layer):
    nb = DEC_BATCH
    lrow = pl.ds(layer, 1)
    x = x_ref[...]
    mod = lambda k: ada_ref[k]
    proj = _mixer_front(x, mod, win_ref)
    col = lambda i: proj[:, i * G:(i + 1) * G]
    slab = lambda a, t: a[t * nb:(t + 1) * nb, :]

    u = _gelu(col(0))
    v = _ln(_gelu(col(1)), sgug_ref[lrow, :], sgulb_ref[lrow, :])
    for t in range(DEC_SEQ):
        sguv_ref[t] = slab(v, t)
    mixed = []
    for t in range(DEC_SEQ):
        m = jnp.zeros((nb, G), F32) + sgub_ref[pl.ds(t, 1), :]
        for s in range(t + 1):
            m = m + sguw_ref[t, pl.ds(s, 1), :] * slab(v, s)
        mixed.append(m)
    ya = u * jnp.concatenate(mixed, axis=0)

    xb = col(2)
    e = [poolin_ref[i] for i in range(POOL_BUF)] + [slab(xb, t) for t in range(DEC_SEQ)]
    n_e = len(e)
    s2 = [e[i] + e[i - 1] if i >= 1 else None for i in range(n_e)]
    s4 = [s2[i] + s2[i - 2] if i >= 3 else None for i in range(n_e)]
    s8 = [s4[i] + s4[i - 4] if i >= 7 else None for i in range(n_e)]
    s16 = [s8[i] + s8[i - 8] if i >= 15 else None for i in range(n_e)]
    grp, win = _pool_lane_consts()
    cnt = jnp.minimum(win, float(PAST_LEN + 1))
    sel = jnp.concatenate([_pool_select(grp, s2[i], s4[i], s8[i], s16[i])
                           for i in range(POOL_BUF, n_e)], axis=0)
    pooled = (sel / cnt - xb).astype(BF16)
    yb = _dot(pooled, poolw_ref[...]) * pools_ref[lrow, :]
    for i in range(POOL_BUF):
        poolst_ref[i] = e[n_e - POOL_BUF + i]

    zc = col(4) * col(5)
    es = [sconvin_ref[i] for i in range(SCONV_W - 1)] + [slab(zc, t) for t in range(DEC_SEQ)]
    yconv = jnp.concatenate(
        [sum(sconvw_ref[layer, pl.ds(k, 1), :] * es[t + k] for k in range(SCONV_W))
         for t in range(DEC_SEQ)], axis=0)
    yc = col(3) * yconv
    for i in range(SCONV_W - 1):
        sconvst_ref[i] = es[len(es) - (SCONV_W - 1) + i]

    zd = col(6) * jax.nn.sigmoid(col(7))
    for i in range(DCONV_W - 1 - DEC_SEQ):
        dconvst_ref[i] = dconvin_ref[i + DEC_SEQ]
    for t in range(DEC_SEQ):
        dconvst_ref[DCONV_W - 1 - DEC_SEQ + t] = slab(zd, t)
    conv = []
    for t in range(DEC_SEQ):
        a = jnp.zeros((nb, G), F32)
        for k in range(DCONV_W):
            i = t + k
            src = dconvin_ref[i] if i < DCONV_W - 1 else slab(zd, i - (DCONV_W - 1))
            a = a + dconvw_ref[layer, pl.ds(k, 1), :] * src
        conv.append(a)
    hd = _ln(jnp.concatenate(conv, axis=0) + dconvb_ref[lrow, :], cvg_ref[lrow, :], cvb_ref[lrow, :])
    yd = _silu(hd)

    mixn = _mixer_norm(ya, yb, yc, yd, onorm_ref, layer)
    o_ref[...] = _mixer_back(x, mixn, mod, wout_ref, lng_ref, lnb_ref, layer)


def _mixer_sample_call(x2d, ada, w_in, w_out, sgu_wsm, sgu_bias, small, pool_in, sconv_in, dconv_in,
                       *, layer):
    (sgu_ln_g, sgu_ln_b, pool_bd, pool_scale, sconv_w, dconv_w, dconv_b, conv_ln_g, conv_ln_b,
     out_norm_g, post_g, post_b) = small
    rows = DEC_SEQ * DEC_BATCH
    full2 = lambda a: _const_spec(a.shape, (0,) * a.ndim)
    st = lambda n: _const_spec((None, n, DEC_BATCH, G), (layer, 0, 0, 0))
    st_out = lambda n: pl.BlockSpec((n, DEC_BATCH, G), lambda i: (0, 0, 0))
    return pl.pallas_call(
        functools.partial(_mixer_sample_kernel, layer=layer),
        grid=(1,),
        in_specs=[
            _const_spec((rows, D_MODEL), (0, 0)),
            _const_spec((None, N_ADA, DEC_BATCH, D_MODEL), (layer, 0, 0, 0)),
            _const_spec((D_MODEL, IN_COLS), (0, 0)),
            _const_spec((D_MODEL, D_MODEL), (0, 0)),
            _const_spec((None, DEC_SEQ, DEC_SEQ, G), (layer, 0, 0, 0)),
            _const_spec((None, DEC_SEQ, G), (layer, 0, 0)),
            full2(sgu_ln_g), full2(sgu_ln_b),
            _const_spec((None, G, G), (layer, 0, 0)),
            full2(pool_scale), full2(sconv_w), full2(dconv_w), full2(dconv_b),
            full2(conv_ln_g), full2(conv_ln_b), full2(out_norm_g), full2(post_g), full2(post_b),
            st(POOL_BUF), st(SCONV_W - 1), st(DCONV_W - 1),
        ],
        out_specs=[pl.BlockSpec((rows, D_MODEL), lambda i: (0, 0)),
                   st_out(DEC_SEQ), st_out(POOL_BUF), st_out(SCONV_W - 1), st_out(DCONV_W - 1)],
        out_shape=[
            jax.ShapeDtypeStruct((rows, D_MODEL), F32),
            jax.ShapeDtypeStruct((DEC_SEQ, DEC_BATCH, G), F32),
            jax.ShapeDtypeStruct((POOL_BUF, DEC_BATCH, G), F32),
            jax.ShapeDtypeStruct((SCONV_W - 1, DEC_BATCH, G), F32),
            jax.ShapeDtypeStruct((DCONV_W - 1, DEC_BATCH, G), F32),
        ],
        compiler_params=pltpu.CompilerParams(
            dimension_semantics=("arbitrary",), vmem_limit_bytes=VMEM_LIMIT_BYTES),
        name=f"mixer_s{layer}",
    )(x2d, ada, w_in, w_out, sgu_wsm, sgu_bias, sgu_ln_g, sgu_ln_b, pool_bd, pool_scale, sconv_w,
      dconv_w, dconv_b, conv_ln_g, conv_ln_b, out_norm_g, post_g, post_b, pool_in, sconv_in, dconv_in)


def kernel(x_prompt, x_sample, state_pool, state_sconv, state_dconv, c_prompt, c_sample, ln_in_g, ln_in_b, w_ada, b_ada, ffn1_w_gate, ffn1_w_up, ffn1_w_down, w_in, sgu_ln_g, sgu_ln_b, sgu_w, sgu_b, pool_w, pool_scale, sconv_w, dconv_w, dconv_b, conv_ln_g, conv_ln_b, out_norm_g, w_out, ffn2_w_gate, ffn2_w_up, ffn2_w_down, post_ln_g, post_ln_b):
    eye = jnp.eye(len(POOL_WINDOWS), dtype=F32)
    pool_bd = jnp.einsum('lgcd,gh->lgchd', pool_w, eye).reshape(DEPTH, G, G).astype(BF16)
    sgu_bias = jnp.repeat(jnp.transpose(sgu_b, (0, 2, 1)), HEAD_DIM, axis=-1)
    sgu_wsm = jnp.repeat(jnp.transpose(sgu_w[:, :, :DEC_SEQ, :DEC_SEQ], (0, 2, 3, 1)),
                         HEAD_DIM, axis=-1)
    in_g, in_b = ln_in_g.reshape(1, D_MODEL), ln_in_b.reshape(1, D_MODEL)
    small = (sgu_ln_g, sgu_ln_b, pool_bd, pool_scale, sconv_w, dconv_w, dconv_b, conv_ln_g,
             conv_ln_b, out_norm_g, post_ln_g, post_ln_b)

    ada = _ada_call(jnp.concatenate([c_sample, c_prompt], axis=0), w_ada, b_ada)

    xs = jnp.transpose(x_sample, (1, 0, 2)).reshape(DEC_SEQ * DEC_BATCH, D_MODEL)
    pool_in = jnp.transpose(state_pool, (0, 2, 1, 3))
    sconv_in = jnp.transpose(state_sconv, (0, 2, 1, 3))
    dconv_in = jnp.transpose(state_dconv, (0, 2, 1, 3))
    xp = x_prompt.reshape(BATCH * SEQ, D_MODEL)

    ffn1 = [w[0].astype(BF16) for w in (ffn1_w_gate, ffn1_w_up, ffn1_w_down)]
    ffn = functools.partial(_ffn_both_call, post_g=post_ln_g, post_b=post_ln_b, in_g=in_g, in_b=in_b)
    outs_p, outs_s = [], []
    for l in range(DEPTH):
        last = l == DEPTH - 1
        side = [(w_in, l), (w_out, l), (ffn2_w_gate, l), (ffn2_w_up, l), (ffn2_w_down, l)]
        xp, xs, (w_in_b, w_out_b, *ffn2) = ffn(xp, xs, ada, *ffn1, layer=l, sub=0, first=(l == 0), side=side)
        xp, *st_p = _mixer_prompt_call(xp, ada, w_in_b, w_out_b, sgu_w, sgu_bias, small, layer=l)
        xs, *st_s = _mixer_sample_call(xs, ada, w_in_b, w_out_b, sgu_wsm, sgu_bias, small,
                                       pool_in, sconv_in, dconv_in, layer=l)
        side = [] if last else [(ffn1_w_gate, l + 1), (ffn1_w_up, l + 1), (ffn1_w_down, l + 1)]
        xp, xs, ffn1 = ffn(xp, xs, ada, *ffn2, layer=l, sub=2, first=False, side=side)
        outs_p.append(st_p)
        outs_s.append(st_s)

    y_prompt = xp.reshape(BATCH, SEQ, D_MODEL)
    y_sample = jnp.transpose(xs.reshape(DEC_SEQ, DEC_BATCH, D_MODEL), (1, 0, 2))
    stack_p = lambda i: jnp.stack([o[i] for o in outs_p], axis=0)
    stack_s = lambda i: jnp.transpose(jnp.stack([o[i] for o in outs_s], axis=0), (0, 2, 1, 3))
    return (y_prompt, y_sample, stack_p(0), stack_s(0), stack_p(1), stack_s(1),
            stack_p(2), stack_s(2), stack_p(3), stack_s(3))
```

```python
import functools

import jax
import jax.numpy as jnp
from jax import lax
from jax.experimental import pallas as pl
from jax.experimental.pallas import tpu as pltpu

D_MODEL = 1024
BATCH = 8
SEQ = 2048
DEPTH = 2
DEC_BATCH = 128
DEC_SEQ = 8
PAST_LEN = 16384
G = 256
HEAD_DIM = 64
A_HEADS = G // HEAD_DIM
CHUNK = 128
POOL_WINDOWS = (2, 4, 8, 16)
POOL_GDIM = G // len(POOL_WINDOWS)
POOL_BUF = 15
SCONV_W = 3
DCONV_W = 31
D_FF = 2816
N_ADA = 9
ALPHA = (2 * DEPTH) ** 0.25
IN_COLS = 8 * G
N_SEQ_ALL = DEC_BATCH + BATCH
ADA_PER_STEP = 3

SUBLANES = 8
BF16_SUBLANES = 16
POOL_CARRY = 16
SCONV_CARRY = 16
DCONV_CARRY = 32
VMEM_LIMIT_BYTES = 56 * 1024 * 1024

TM_FFN = 1024
ROW_PARTS = 2
TM_MIX = 1024
SUB_MIX = 512
W_CHUNKS = 16

BF16 = jnp.bfloat16
F32 = jnp.float32


def _ln(x, g, b, eps=1e-5):
    mu = jnp.mean(x, axis=-1, keepdims=True)
    xc = x - mu
    var = jnp.mean(xc * xc, axis=-1, keepdims=True)
    return xc * lax.rsqrt(var + eps) * g + b


def _gelu(x):
    return 0.5 * x * (1.0 + lax.erf(x * 0.7071067811865476))


def _silu(x):
    return x * jax.nn.sigmoid(x)


def _rows(x, m):
    if m.shape[0] == 1 or m.shape[0] == x.shape[0]:
        return m
    reps = x.shape[0] // m.shape[0]
    return jnp.broadcast_to(m[None], (reps,) + m.shape).reshape(x.shape)


def _dot(a, b):
    return jnp.dot(a, b, preferred_element_type=F32)


def _const_spec(shape, index):
    return pl.BlockSpec(shape, lambda *_: index, pipeline_mode=pl.Buffered(1))


def _ada_kernel(c_ref, w_ref, b_ref, o_ref):
    s = _silu(c_ref[...]).astype(BF16)
    y = _dot(s, w_ref[...].astype(BF16))
    for k in range(ADA_PER_STEP):
        cols = slice(k * D_MODEL, (k + 1) * D_MODEL)
        o_ref[k] = y[:, cols] + b_ref[pl.ds(pl.program_id(0), 1), cols]


def _ada_call(c_all, w_ada, b_ada):
    return pl.pallas_call(
        _ada_kernel,
        grid=(DEPTH, N_ADA // ADA_PER_STEP),
        in_specs=[
            pl.BlockSpec((N_SEQ_ALL, D_MODEL), lambda l, k: (0, 0)),
            pl.BlockSpec((None, D_MODEL, ADA_PER_STEP * D_MODEL), lambda l, k: (l, 0, k)),
            pl.BlockSpec((DEPTH, ADA_PER_STEP * D_MODEL), lambda l, k: (0, k)),
        ],
        out_specs=pl.BlockSpec((None, ADA_PER_STEP, N_SEQ_ALL, D_MODEL), lambda l, k: (l, k, 0, 0)),
        out_shape=jax.ShapeDtypeStruct((DEPTH, N_ADA, N_SEQ_ALL, D_MODEL), F32),
        compiler_params=pltpu.CompilerParams(
            dimension_semantics=("arbitrary", "arbitrary"), vmem_limit_bytes=VMEM_LIMIT_BYTES),
        name="ada",
    )(c_all, w_ada, b_ada)


def _ffn_math(x, mod, wg_ref, wu_ref, wd_ref, lng_ref, lnb_ref, *, layer, sub):
    k0 = 0 if sub == 0 else 6
    sh, sc, gt = mod(k0), mod(k0 + 1), mod(k0 + 2)
    xm = (x * (1.0 + _rows(x, sc)) + _rows(x, sh)).astype(BF16)
    hg = _dot(xm, wg_ref[...])
    hu = _dot(xm, wu_ref[...])
    h = (_silu(hg) * hu).astype(BF16)
    gain = 0.5 * (1.0 + _rows(x, gt))
    half = x.shape[0] // 2
    outs = []
    for r in (slice(0, half), slice(half, 2 * half)):
        y = ALPHA * x[r] + (gain if gain.shape[0] == 1 else gain[r]) * _dot(h[r], wd_ref[...])
        outs.append(_ln(y, lng_ref[layer, pl.ds(sub, 1), :], lnb_ref[layer, pl.ds(sub, 1), :]))
    return jnp.concatenate(outs, axis=0)


def _ffn_both_kernel(x_hbm, xs_hbm, adap_ref, adas_ref, wg_ref, wu_ref, wd_ref, lng_ref, lnb_ref,
                     ing_ref, inb_ref, *rest, layer, sub, first, n_side, specs, own_weights):
    if own_weights:
        *rest, wg_s, wu_s, wd_s = rest

        def to_bf16(*chunks):
            c = pl.program_id(0)
            for src, dst in zip(chunks, (wg_s, wu_s, wd_s)):
                n = src.shape[0]
                dst[pl.ds(pl.multiple_of(c * n, n), n), :] = src[...].astype(BF16)

        chunk = lambda w: pl.BlockSpec((None, w.shape[1] // W_CHUNKS, w.shape[2]), lambda c: (layer, c, 0))
        pltpu.emit_pipeline(to_bf16, grid=(W_CHUNKS,), in_specs=[chunk(wg_ref), chunk(wu_ref), chunk(wd_ref)],
                            out_specs=[])(wg_ref, wu_ref, wd_ref)
        wg_ref, wu_ref, wd_ref = wg_s, wu_s, wd_s
    side_in_hbm, (o_hbm, os_hbm), side_out_hbm = rest[:n_side], rest[n_side:n_side + 2], rest[n_side + 2:]
    nt = SEQ // TM_FFN
    part = TM_FFN // ROW_PARTS

    def half_step(x, mod):
        if first:
            x = _ln(x, ing_ref[...], inb_ref[...])
        return _ffn_math(x, mod, wg_ref, wu_ref, wd_ref, lng_ref, lnb_ref, layer=layer, sub=sub)

    def prompt_body(x_ref, *tiles):
        side_in, o_ref, side_out = tiles[:n_side], tiles[n_side], tiles[n_side + 1:]
        b = lax.div(pl.program_id(0), nt)
        mod = lambda k: adap_ref[k % 3, pl.ds(b, 1), :]
        for i in range(ROW_PARTS):
            rows = slice(i * part, (i + 1) * part)
            o_ref[rows, :] = half_step(x_ref[rows, :], mod)
        for src, dst in zip(side_in, side_out):
            dst[...] = src[...].astype(BF16)

    def sample_body(x_ref, o_ref):
        o_ref[...] = half_step(x_ref[...], lambda k: adas_ref[k % 3])

    in_specs, out_specs, s_spec = specs
    pltpu.emit_pipeline(prompt_body, grid=(BATCH * nt,), in_specs=in_specs, out_specs=out_specs)(
        x_hbm, *side_in_hbm, o_hbm, *side_out_hbm)
    pltpu.emit_pipeline(sample_body, grid=(xs_hbm.shape[0] // s_spec.block_shape[0],),
                        in_specs=[s_spec], out_specs=[s_spec])(xs_hbm, os_hbm)


def _ffn_both_call(x2d, xs2d, ada, wg, wu, wd, post_g, post_b, in_g, in_b, *, layer, sub, first, side=()):
    n_steps = BATCH * (SEQ // TM_FFN)
    x_spec = pl.BlockSpec((TM_FFN, D_MODEL), lambda s: (s, 0))
    s_spec = pl.BlockSpec((TM_FFN // ROW_PARTS, D_MODEL), lambda s: (s, 0))
    k0 = 0 if sub == 0 else 6
    side_in_specs, side_out_specs, side_shapes = [], [], []
    for w, wl in side:
        k_dim, n_dim = w.shape[1:]
        steps = n_steps
        while (k_dim // steps) % BF16_SUBLANES:
            steps //= 2
        per = n_steps // steps
        side_in_specs.append(pl.BlockSpec((None, k_dim // steps, n_dim),
                                          lambda s, wl=wl, per=per: (wl, s // per, 0)))
        side_out_specs.append(pl.BlockSpec((k_dim // steps, n_dim), lambda s, per=per: (s // per, 0)))
        side_shapes.append(jax.ShapeDtypeStruct((k_dim, n_dim), BF16))
    in_hbm = pl.BlockSpec(memory_space=pl.ANY)
    own_weights = wg.ndim == 3
    kern = functools.partial(_ffn_both_kernel, layer=layer, sub=sub, first=first, n_side=len(side),
                             specs=([x_spec] + side_in_specs, [x_spec] + side_out_specs, s_spec),
                             own_weights=own_weights)
    whole = lambda a: _const_spec(a.shape, (0,) * a.ndim)
    res = pl.pallas_call(
        kern,
        grid=(1,),
        in_specs=[in_hbm, in_hbm,
                  _const_spec((None, 3, BATCH, D_MODEL), (layer, k0 // 3, DEC_BATCH // BATCH, 0)),
                  _const_spec((None, 3, DEC_BATCH, D_MODEL), (layer, k0 // 3, 0, 0))]
                 + [in_hbm if own_weights else whole(w) for w in (wg, wu, wd)]
                 + [whole(a) for a in (post_g, post_b, in_g, in_b)]
                 + [in_hbm] * len(side),
        scratch_shapes=[pltpu.VMEM(w.shape[1:], BF16) for w in (wg, wu, wd)] if own_weights else [],
        out_specs=[in_hbm] * (2 + len(side)),
        out_shape=[jax.ShapeDtypeStruct(x2d.shape, F32), jax.ShapeDtypeStruct(xs2d.shape, F32)] + side_shapes,
        compiler_params=pltpu.CompilerParams(vmem_limit_bytes=VMEM_LIMIT_BYTES),
        name=f"ffn{sub}_{layer}",
    )(x2d, xs2d, ada, ada, wg, wu, wd, post_g, post_b, in_g, in_b, *[w for w, _ in side])
    return res[0], res[1], list(res[2:])


def _pool_lane_consts():
    lane = lax.broadcasted_iota(jnp.int32, (1, G), 1)
    grp = lane // POOL_GDIM
    win = jnp.where(grp == 0, float(POOL_WINDOWS[0]),
                    jnp.where(grp == 1, float(POOL_WINDOWS[1]),
                              jnp.where(grp == 2, float(POOL_WINDOWS[2]), float(POOL_WINDOWS[3]))))
    return grp, win


def _pool_select(grp, s2, s4, s8, s16):
    return jnp.where(grp == 0, s2, jnp.where(grp == 1, s4, jnp.where(grp == 2, s8, s16)))


def _rms_group(y, g, eps=1e-6):
    return y * lax.rsqrt(jnp.mean(y * y, axis=-1, keepdims=True) + eps) * g


def _mixer_front(x, mod, win_ref):
    sh, sc = mod(3), mod(4)
    xm = (x * (1.0 + _rows(x, sc)) + _rows(x, sh)).astype(BF16)
    return _dot(xm, win_ref[...])


def _mixer_norm(ya, yb, yc, yd, onorm_ref, layer):
    parts = [ya, yb, yc, yd]
    normed = [_rms_group(p, onorm_ref[pl.ds(layer, 1), i * G:(i + 1) * G]).astype(BF16)
              for i, p in enumerate(parts)]
    return jnp.concatenate(normed, axis=-1)


def _mixer_back(x, mixn, mod, wout_ref, lng_ref, lnb_ref, layer):
    y = ALPHA * x + (1.0 + _rows(x, mod(5))) * _dot(mixn, wout_ref[...])
    return _ln(y, lng_ref[layer, pl.ds(1, 1), :], lnb_ref[layer, pl.ds(1, 1), :])


def _mixer_prompt_mid(proj, pos0, r0, last, sguw_ref, sgub_ref, sgug_ref, sgulb_ref,
                      poolw_ref, pools_ref, sconvw_ref, dconvw_ref, dconvb_ref, cvg_ref, cvb_ref,
                      onorm_ref, sguv_ref, poolst_ref, sconvst_ref, dconvst_ref,
                      pool_ext, sconv_ext, dconv_ext, *, layer):
    n = proj.shape[0]
    lrow = pl.ds(layer, 1)
    col = lambda i: proj[:, i * G:(i + 1) * G]

    u = _gelu(col(0))
    v = _ln(_gelu(col(1)), sgug_ref[lrow, :], sgulb_ref[lrow, :])
    if last:
        sguv_ref[...] = v[n - CHUNK:, :]
    r_i = lax.broadcasted_iota(jnp.int32, (CHUNK, CHUNK), 0)
    c_i = lax.broadcasted_iota(jnp.int32, (CHUNK, CHUNK), 1)
    wcat = jnp.concatenate([jnp.where(r_i >= c_i, sguw_ref[h], 0.0) for h in range(A_HEADS)],
                           axis=1).astype(BF16)
    head = lax.broadcasted_iota(jnp.int32, (1, G), 1) // HEAD_DIM
    bias = sgub_ref[...]
    mixed = []
    for c in range(n // CHUNK):
        vc = v[c * CHUNK:(c + 1) * CHUNK, :]
        rhs = jnp.concatenate([jnp.where(head == h, vc, 0.0) for h in range(A_HEADS)],
                              axis=0).astype(BF16)
        mixed.append(_dot(wcat, rhs) + bias)
    ya = u * jnp.concatenate(mixed, axis=0)

    xb = col(2)
    pool_ext[r0 + POOL_CARRY:r0 + POOL_CARRY + n, :] = xb
    e = pool_ext[r0:r0 + POOL_CARRY + n, :]
    s2 = e + pltpu.roll(e, 1, 0)
    s4 = s2 + pltpu.roll(s2, 2, 0)
    s8 = s4 + pltpu.roll(s4, 4, 0)
    s16 = s8 + pltpu.roll(s8, 8, 0)
    grp, win = _pool_lane_consts()
    ssel = _pool_select(grp, s2, s4, s8, s16)[POOL_CARRY:, :]
    pos = pos0 + lax.broadcasted_iota(jnp.int32, (n, 1), 0)
    cnt = jnp.minimum(win, (pos + 1).astype(F32))
    pooled = (ssel / cnt - xb).astype(BF16)
    yb = _dot(pooled, poolw_ref[...]) * pools_ref[lrow, :]
    if last:
        poolst_ref[...] = pool_ext[r0 + n + POOL_CARRY - POOL_BUF:r0 + n + POOL_CARRY, :]

    zc = col(4) * col(5)
    sconv_ext[r0 + SCONV_CARRY:r0 + SCONV_CARRY + n, :] = zc
    es = sconv_ext[r0:r0 + SCONV_CARRY + n, :]
    yconv = (sconvw_ref[layer, pl.ds(2, 1), :] * es
             + sconvw_ref[layer, pl.ds(1, 1), :] * pltpu.roll(es, 1, 0)
             + sconvw_ref[layer, pl.ds(0, 1), :] * pltpu.roll(es, 2, 0))
    yc = col(3) * yconv[SCONV_CARRY:, :]
    if last:
        sconvst_ref[...] = sconv_ext[r0 + n + SCONV_CARRY - (SCONV_W - 1):r0 + n + SCONV_CARRY, :]

    zd = col(6) * jax.nn.sigmoid(col(7))
    dconv_ext[r0 + DCONV_CARRY:r0 + DCONV_CARRY + n, :] = zd
    if last:
        dconvst_ref[...] = dconv_ext[r0 + n + DCONV_CARRY - (DCONV_W - 1):r0 + n + DCONV_CARRY, :]
    ed = dconv_ext[r0:r0 + DCONV_CARRY + n, :]
    acc = jnp.zeros((n, G), F32)
    for r in range(SUBLANES):
        er = ed if r == 0 else pltpu.roll(ed, r, 0)
        for q in range(DCONV_CARRY // SUBLANES):
            tap = SUBLANES * q + r
            if tap >= DCONV_W:
                continue
            lo = DCONV_CARRY - SUBLANES * q
            acc = acc + dconvw_ref[layer, pl.ds(DCONV_W - 1 - tap, 1), :] * er[lo:lo + n, :]
    hd = _ln(acc + dconvb_ref[lrow, :], cvg_ref[lrow, :], cvb_ref[lrow, :])
    yd = _silu(hd)

    return _mixer_norm(ya, yb, yc, yd, onorm_ref, layer)


def _mixer_prompt_kernel(x_ref, ada_ref, win_ref, wout_ref, sguw_ref, sgub_ref, sgug_ref, sgulb_ref,
                         poolw_ref, pools_ref, sconvw_ref, dconvw_ref, dconvb_ref, cvg_ref, cvb_ref,
                         onorm_ref, lng_ref, lnb_ref,
                         o_ref, sguv_ref, poolst_ref, sconvst_ref, dconvst_ref,
                         pool_ext, sconv_ext, dconv_ext, *, layer, sub_rows):
    tm = x_ref.shape[0]
    b = pl.program_id(0)
    j = pl.program_id(1)

    @pl.when(j == 0)
    def _():
        pool_ext[0:POOL_CARRY, :] = jnp.zeros((POOL_CARRY, G), F32)
        sconv_ext[0:SCONV_CARRY, :] = jnp.zeros((SCONV_CARRY, G), F32)
        dconv_ext[0:DCONV_CARRY, :] = jnp.zeros((DCONV_CARRY, G), F32)

    @pl.when(j > 0)
    def _():
        pool_ext[0:POOL_CARRY, :] = pool_ext[tm:tm + POOL_CARRY, :]
        sconv_ext[0:SCONV_CARRY, :] = sconv_ext[tm:tm + SCONV_CARRY, :]
        dconv_ext[0:DCONV_CARRY, :] = dconv_ext[tm:tm + DCONV_CARRY, :]

    mod = lambda k: ada_ref[k, pl.ds(b, 1), :]
    n_sub = tm // sub_rows
    rows = lambda h: slice(h * sub_rows, (h + 1) * sub_rows)
    front = lambda h: _mixer_front(x_ref[rows(h), :], mod, win_ref)
    proj = front(0)
    for h in range(n_sub):
        proj_next = front(h + 1) if h + 1 < n_sub else None
        mixn = _mixer_prompt_mid(
            proj, j * tm + h * sub_rows, h * sub_rows, h == n_sub - 1,
            sguw_ref, sgub_ref, sgug_ref, sgulb_ref, poolw_ref, pools_ref, sconvw_ref,
            dconvw_ref, dconvb_ref, cvg_ref, cvb_ref, onorm_ref,
            sguv_ref, poolst_ref, sconvst_ref, dconvst_ref, pool_ext, sconv_ext, dconv_ext, layer=layer)
        o_ref[rows(h), :] = _mixer_back(x_ref[rows(h), :], mixn, mod, wout_ref, lng_ref, lnb_ref, layer)
        proj = proj_next


def _mixer_prompt_call(x2d, ada, w_in, w_out, sgu_w, sgu_bias, small, *, layer):
    (sgu_ln_g, sgu_ln_b, pool_bd, pool_scale, sconv_w, dconv_w, dconv_b, conv_ln_g, conv_ln_b,
     out_norm_g, post_g, post_b) = small
    tm = TM_MIX
    nt = SEQ // tm
    x_spec = pl.BlockSpec((tm, D_MODEL), lambda b, j: (b * nt + j, 0))
    st_spec = lambda n: pl.BlockSpec((None, n, G), lambda b, j: (b, 0, 0))
    full2 = lambda a: _const_spec(a.shape, (0,) * a.ndim)
    return pl.pallas_call(
        functools.partial(_mixer_prompt_kernel, layer=layer, sub_rows=SUB_MIX),
        grid=(BATCH, nt),
        in_specs=[
            x_spec,
            _const_spec((None, N_ADA, BATCH, D_MODEL), (layer, 0, DEC_BATCH // BATCH, 0)),
            _const_spec((D_MODEL, IN_COLS), (0, 0)),
            _const_spec((D_MODEL, D_MODEL), (0, 0)),
            _const_spec((None, A_HEADS, CHUNK, CHUNK), (layer, 0, 0, 0)),
            _const_spec((None, CHUNK, G), (layer, 0, 0)),
            full2(sgu_ln_g), full2(sgu_ln_b),
            _const_spec((None, G, G), (layer, 0, 0)),
            full2(pool_scale), full2(sconv_w), full2(dconv_w), full2(dconv_b),
            full2(conv_ln_g), full2(conv_ln_b), full2(out_norm_g), full2(post_g), full2(post_b),
        ],
        out_specs=[x_spec, st_spec(CHUNK), st_spec(POOL_BUF), st_spec(SCONV_W - 1), st_spec(DCONV_W - 1)],
        out_shape=[
            jax.ShapeDtypeStruct((BATCH * SEQ, D_MODEL), F32),
            jax.ShapeDtypeStruct((BATCH, CHUNK, G), F32),
            jax.ShapeDtypeStruct((BATCH, POOL_BUF, G), F32),
            jax.ShapeDtypeStruct((BATCH, SCONV_W - 1, G), F32),
            jax.ShapeDtypeStruct((BATCH, DCONV_W - 1, G), F32),
        ],
        scratch_shapes=[
            pltpu.VMEM((tm + POOL_CARRY, G), F32),
            pltpu.VMEM((tm + SCONV_CARRY, G), F32),
            pltpu.VMEM((tm + DCONV_CARRY, G), F32),
        ],
        compiler_params=pltpu.CompilerParams(
            dimension_semantics=("arbitrary", "arbitrary"), vmem_limit_bytes=VMEM_LIMIT_BYTES),
        name=f"mixer_p{layer}",
    )(x2d, ada, w_in, w_out, sgu_w, sgu_bias, sgu_ln_g, sgu_ln_b, pool_bd, pool_scale, sconv_w,
      dconv_w, dconv_b, conv_ln_g, conv_ln_b, out_norm_g, post_g, post_b)


def _mixer_sample_kernel(x_ref, ada_ref, win_ref, wout_ref, sguw_ref, sgub_ref, sgug_ref, sgulb_ref,
                         poolw_ref, pools_ref, sconvw_ref, dconvw_ref, dconvb_ref, cvg_ref, cvb_ref,
                         onorm_ref, lng_ref, lnb_ref, poolin_ref, sconvin_ref, dconvin_ref,
                         o_ref, sguv_ref, poolst_ref, sconvst_ref, dconvst_ref, *, layer):
    nb = DEC_BATCH
    lrow = pl.ds(layer, 1)
    x = x_ref[...]
    mod = lambda k: ada_ref[k]
    proj = _mixer_front(x, mod, win_ref)
    col = lambda i: proj[:, i * G:(i + 1) * G]
    slab = lambda a, t: a[t * nb:(t + 1) * nb, :]

    u = _gelu(col(0))
    v = _ln(_gelu(col(1)), sgug_ref[lrow, :], sgulb_ref[lrow, :])
    for t in range(DEC_SEQ):
        sguv_ref[t] = slab(v, t)
    mixed = []
    for t in range(DEC_SEQ):
        m = jnp.zeros((nb, G), F32) + sgub_ref[pl.ds(t, 1), :]
        for s in range(t + 1):
            m = m + sguw_ref[t, pl.ds(s, 1), :] * slab(v, s)
        mixed.append(m)
    ya = u * jnp.concatenate(mixed, axis=0)

    xb = col(2)
    e = [poolin_ref[i] for i in range(POOL_BUF)] + [slab(xb, t) for t in range(DEC_SEQ)]
    n_e = len(e)
    s2 = [e[i] + e[i - 1] if i >= 1 else None for i in range(n_e)]
    s4 = [s2[i] + s2[i - 2] if i >= 3 else None for i in range(n_e)]
    s8 = [s4[i] + s4[i - 4] if i >= 7 else None for i in range(n_e)]
    s16 = [s8[i] + s8[i - 8] if i >= 15 else None for i in range(n_e)]
    grp, win = _pool_lane_consts()
    cnt = jnp.minimum(win, float(PAST_LEN + 1))
    sel = jnp.concatenate([_pool_select(grp, s2[i], s4[i], s8[i], s16[i])
                           for i in range(POOL_BUF, n_e)], axis=0)
    pooled = (sel / cnt - xb).astype(BF16)
    yb = _dot(pooled, poolw_ref[...]) * pools_ref[lrow, :]
    for i in range(POOL_BUF):
        poolst_ref[i] = e[n_e - POOL_BUF + i]

    zc = col(4) * col(5)
    es = [sconvin_ref[i] for i in range(SCONV_W - 1)] + [slab(zc, t) for t in range(DEC_SEQ)]
    yconv = jnp.concatenate(
        [sum(sconvw_ref[layer, pl.ds(k, 1), :] * es[t + k] for k in range(SCONV_W))
         for t in range(DEC_SEQ)], axis=0)
    yc = col(3) * yconv
    for i in range(SCONV_W - 1):
        sconvst_ref[i] = es[len(es) - (SCONV_W - 1) + i]

    zd = col(6) * jax.nn.sigmoid(col(7))
    for i in range(DCONV_W - 1 - DEC_SEQ):
        dconvst_ref[i] = dconvin_ref[i + DEC_SEQ]
    for t in range(DEC_SEQ):
        dconvst_ref[DCONV_W - 1 - DEC_SEQ + t] = slab(zd, t)
    conv = []
    for t in range(DEC_SEQ):
        a = jnp.zeros((nb, G), F32)
        for k in range(DCONV_W):
            i = t + k
            src = dconvin_ref[i] if i < DCONV_W - 1 else slab(zd, i - (DCONV_W - 1))
            a = a + dconvw_ref[layer, pl.ds(k, 1), :] * src
        conv.append(a)
    hd = _ln(jnp.concatenate(conv, axis=0) + dconvb_ref[lrow, :], cvg_ref[lrow, :], cvb_ref[lrow, :])
    yd = _silu(hd)

    mixn = _mixer_norm(ya, yb, yc, yd, onorm_ref, layer)
    o_ref[...] = _mixer_back(x, mixn, mod, wout_ref, lng_ref, lnb_ref, layer)


def _mixer_sample_call(x2d, ada, w_in, w_out, sgu_wsm, sgu_bias, small, pool_in, sconv_in, dconv_in,
                       *, layer):
    (sgu_ln_g, sgu_ln_b, pool_bd, pool_scale, sconv_w, dconv_w, dconv_b, conv_ln_g, conv_ln_b,
     out_norm_g, post_g, post_b) = small
    rows = DEC_SEQ * DEC_BATCH
    full2 = lambda a: _const_spec(a.shape, (0,) * a.ndim)
    st = lambda n: _const_spec((None, n, DEC_BATCH, G), (layer, 0, 0, 0))
    st_out = lambda n: pl.BlockSpec((n, DEC_BATCH, G), lambda i: (0, 0, 0))
    return pl.pallas_call(
        functools.partial(_mixer_sample_kernel, layer=layer),
        grid=(1,),
        in_specs=[
            _const_spec((rows, D_MODEL), (0, 0)),
            _const_spec((None, N_ADA, DEC_BATCH, D_MODEL), (layer, 0, 0, 0)),
            _const_spec((D_MODEL, IN_COLS), (0, 0)),
            _const_spec((D_MODEL, D_MODEL), (0, 0)),
            _const_spec((None, DEC_SEQ, DEC_SEQ, G), (layer, 0, 0, 0)),
            _const_spec((None, DEC_SEQ, G), (layer, 0, 0)),
            full2(sgu_ln_g), full2(sgu_ln_b),
            _const_spec((None, G, G), (layer, 0, 0)),
            full2(pool_scale), full2(sconv_w), full2(dconv_w), full2(dconv_b),
            full2(conv_ln_g), full2(conv_ln_b), full2(out_norm_g), full2(post_g), full2(post_b),
            st(POOL_BUF), st(SCONV_W - 1), st(DCONV_W - 1),
        ],
        out_specs=[pl.BlockSpec((rows, D_MODEL), lambda i: (0, 0)),
                   st_out(DEC_SEQ), st_out(POOL_BUF), st_out(SCONV_W - 1), st_out(DCONV_W - 1)],
        out_shape=[
            jax.ShapeDtypeStruct((rows, D_MODEL), F32),
            jax.ShapeDtypeStruct((DEC_SEQ, DEC_BATCH, G), F32),
            jax.ShapeDtypeStruct((POOL_BUF, DEC_BATCH, G), F32),
            jax.ShapeDtypeStruct((SCONV_W - 1, DEC_BATCH, G), F32),
            jax.ShapeDtypeStruct((DCONV_W - 1, DEC_BATCH, G), F32),
        ],
        compiler_params=pltpu.CompilerParams(
            dimension_semantics=("arbitrary",), vmem_limit_bytes=VMEM_LIMIT_BYTES),
        name=f"mixer_s{layer}",
    )(x2d, ada, w_in, w_out, sgu_wsm, sgu_bias, sgu_ln_g, sgu_ln_b, pool_bd, pool_scale, sconv_w,
      dconv_w, dconv_b, conv_ln_g, conv_ln_b, out_norm_g, post_g, post_b, pool_in, sconv_in, dconv_in)


def kernel(x_prompt, x_sample, state_pool, state_sconv, state_dconv, c_prompt, c_sample, ln_in_g, ln_in_b, w_ada, b_ada, ffn1_w_gate, ffn1_w_up, ffn1_w_down, w_in, sgu_ln_g, sgu_ln_b, sgu_w, sgu_b, pool_w, pool_scale, sconv_w, dconv_w, dconv_b, conv_ln_g, conv_ln_b, out_norm_g, w_out, ffn2_w_gate, ffn2_w_up, ffn2_w_down, post_ln_g, post_ln_b):
    eye = jnp.eye(len(POOL_WINDOWS), dtype=F32)
    pool_bd = jnp.einsum('lgcd,gh->lgchd', pool_w, eye).reshape(DEPTH, G, G).astype(BF16)
    sgu_bias = jnp.repeat(jnp.transpose(sgu_b, (0, 2, 1)), HEAD_DIM, axis=-1)
    sgu_wsm = jnp.repeat(jnp.transpose(sgu_w[:, :, :DEC_SEQ, :DEC_SEQ], (0, 2, 3, 1)),
                         HEAD_DIM, axis=-1)
    in_g, in_b = ln_in_g.reshape(1, D_MODEL), ln_in_b.reshape(1, D_MODEL)
    small = (sgu_ln_g, sgu_ln_b, pool_bd, pool_scale, sconv_w, dconv_w, dconv_b, conv_ln_g,
             conv_ln_b, out_norm_g, post_ln_g, post_ln_b)

    ada = _ada_call(jnp.concatenate([c_sample, c_prompt], axis=0), w_ada, b_ada)

    xs = jnp.transpose(x_sample, (1, 0, 2)).reshape(DEC_SEQ * DEC_BATCH, D_MODEL)
    pool_in = jnp.transpose(state_pool, (0, 2, 1, 3))
    sconv_in = jnp.transpose(state_sconv, (0, 2, 1, 3))
    dconv_in = jnp.transpose(state_dconv, (0, 2, 1, 3))
    xp = x_prompt.reshape(BATCH * SEQ, D_MODEL)

    ffn1 = [ffn1_w_gate, ffn1_w_up, ffn1_w_down]
    ffn = functools.partial(_ffn_both_call, post_g=post_ln_g, post_b=post_ln_b, in_g=in_g, in_b=in_b)
    outs_p, outs_s = [], []
    for l in range(DEPTH):
        last = l == DEPTH - 1
        side = [(w_in, l), (w_out, l), (ffn2_w_gate, l), (ffn2_w_up, l), (ffn2_w_down, l)]
        xp, xs, (w_in_b, w_out_b, *ffn2) = ffn(xp, xs, ada, *ffn1, layer=l, sub=0, first=(l == 0), side=side)
        xp, *st_p = _mixer_prompt_call(xp, ada, w_in_b, w_out_b, sgu_w, sgu_bias, small, layer=l)
        xs, *st_s = _mixer_sample_call(xs, ada, w_in_b, w_out_b, sgu_wsm, sgu_bias, small,
                                       pool_in, sconv_in, dconv_in, layer=l)
        side = [] if last else [(ffn1_w_gate, l + 1), (ffn1_w_up, l + 1), (ffn1_w_down, l + 1)]
        xp, xs, ffn1 = ffn(xp, xs, ada, *ffn2, layer=l, sub=2, first=False, side=side)
        outs_p.append(st_p)
        outs_s.append(st_s)

    y_prompt = xp.reshape(BATCH, SEQ, D_MODEL)
    y_sample = jnp.transpose(xs.reshape(DEC_SEQ, DEC_BATCH, D_MODEL), (1, 0, 2))
    stack_p = lambda i: jnp.stack([o[i] for o in outs_p], axis=0)
    stack_s = lambda i: jnp.transpose(jnp.stack([o[i] for o in outs_s], axis=0), (0, 2, 1, 3))
    return (y_prompt, y_sample, stack_p(0), stack_s(0), stack_p(1), stack_s(1),
            stack_p(2), stack_s(2), stack_p(3), stack_s(3))
```

```python
import functools

import jax
import jax.numpy as jnp
from jax import lax
from jax.experimental import pallas as pl
from jax.experimental.pallas import tpu as pltpu

D_MODEL = 1024
BATCH = 8
SEQ = 2048
DEPTH = 2
DEC_BATCH = 128
DEC_SEQ = 8
PAST_LEN = 16384
G = 256
HEAD_DIM = 64
A_HEADS = G // HEAD_DIM
CHUNK = 128
POOL_WINDOWS = (2, 4, 8, 16)
POOL_GDIM = G // len(POOL_WINDOWS)
POOL_BUF = 15
SCONV_W = 3
DCONV_W = 31
D_FF = 2816
N_ADA = 9
ALPHA = (2 * DEPTH) ** 0.25
IN_COLS = 8 * G
N_SEQ_ALL = DEC_BATCH + BATCH
ADA_PER_STEP = 3

SUBLANES = 8
BF16_SUBLANES = 16
POOL_CARRY = 16
SCONV_CARRY = 16
DCONV_CARRY = 32
VMEM_LIMIT_BYTES = 56 * 1024 * 1024

TM_FFN = 1024
ROW_PARTS = 2
TM_MIX = 1024
SUB_MIX = 512
W_CHUNKS = 8

BF16 = jnp.bfloat16
F32 = jnp.float32


def _ln(x, g, b, eps=1e-5):
    mu = jnp.mean(x, axis=-1, keepdims=True)
    xc = x - mu
    var = jnp.mean(xc * xc, axis=-1, keepdims=True)
    return xc * lax.rsqrt(var + eps) * g + b


def _gelu(x):
    return 0.5 * x * (1.0 + lax.erf(x * 0.7071067811865476))


def _silu(x):
    return x * jax.nn.sigmoid(x)


def _rows(x, m):
    if m.shape[0] == 1 or m.shape[0] == x.shape[0]:
        return m
    reps = x.shape[0] // m.shape[0]
    return jnp.broadcast_to(m[None], (reps,) + m.shape).reshape(x.shape)


def _dot(a, b):
    return jnp.dot(a, b, preferred_element_type=F32)


def _const_spec(shape, index):
    return pl.BlockSpec(shape, lambda *_: index, pipeline_mode=pl.Buffered(1))


def _ada_kernel(c_ref, w_ref, b_ref, o_ref):
    s = _silu(c_ref[...]).astype(BF16)
    y = _dot(s, w_ref[...].astype(BF16))
    for k in range(ADA_PER_STEP):
        cols = slice(k * D_MODEL, (k + 1) * D_MODEL)
        o_ref[k] = y[:, cols] + b_ref[pl.ds(pl.program_id(0), 1), cols]


def _ada_call(c_all, w_ada, b_ada):
    return pl.pallas_call(
        _ada_kernel,
        grid=(DEPTH, N_ADA // ADA_PER_STEP),
        in_specs=[
            pl.BlockSpec((N_SEQ_ALL, D_MODEL), lambda l, k: (0, 0)),
            pl.BlockSpec((None, D_MODEL, ADA_PER_STEP * D_MODEL), lambda l, k: (l, 0, k)),
            pl.BlockSpec((DEPTH, ADA_PER_STEP * D_MODEL), lambda l, k: (0, k)),
        ],
        out_specs=pl.BlockSpec((None, ADA_PER_STEP, N_SEQ_ALL, D_MODEL), lambda l, k: (l, k, 0, 0)),
        out_shape=jax.ShapeDtypeStruct((DEPTH, N_ADA, N_SEQ_ALL, D_MODEL), F32),
        compiler_params=pltpu.CompilerParams(
            dimension_semantics=("arbitrary", "arbitrary"), vmem_limit_bytes=VMEM_LIMIT_BYTES),
        name="ada",
    )(c_all, w_ada, b_ada)


def _ffn_math(x, mod, wg_ref, wu_ref, wd_ref, lng_ref, lnb_ref, *, layer, sub):
    k0 = 0 if sub == 0 else 6
    sh, sc, gt = mod(k0), mod(k0 + 1), mod(k0 + 2)
    xm = (x * (1.0 + _rows(x, sc)) + _rows(x, sh)).astype(BF16)
    hg = _dot(xm, wg_ref[...])
    hu = _dot(xm, wu_ref[...])
    h = (_silu(hg) * hu).astype(BF16)
    gain = 0.5 * (1.0 + _rows(x, gt))
    half = x.shape[0] // 2
    outs = []
    for r in (slice(0, half), slice(half, 2 * half)):
        y = ALPHA * x[r] + (gain if gain.shape[0] == 1 else gain[r]) * _dot(h[r], wd_ref[...])
        outs.append(_ln(y, lng_ref[layer, pl.ds(sub, 1), :], lnb_ref[layer, pl.ds(sub, 1), :]))
    return jnp.concatenate(outs, axis=0)


def _ffn_both_kernel(x_hbm, xs_hbm, adap_ref, adas_ref, wg_ref, wu_ref, wd_ref, lng_ref, lnb_ref,
                     ing_ref, inb_ref, *rest, layer, sub, first, n_side, specs, own_weights):
    if own_weights:
        *rest, wg_s, wu_s, wd_s = rest

        def to_bf16(*chunks):
            c = pl.program_id(0)
            for src, dst in zip(chunks, (wg_s, wu_s, wd_s)):
                n = src.shape[0]
                dst[pl.ds(pl.multiple_of(c * n, n), n), :] = src[...].astype(BF16)

        chunk = lambda w: pl.BlockSpec((None, w.shape[1] // W_CHUNKS, w.shape[2]), lambda c: (layer, c, 0))
        pltpu.emit_pipeline(to_bf16, grid=(W_CHUNKS,), in_specs=[chunk(wg_ref), chunk(wu_ref), chunk(wd_ref)],
                            out_specs=[])(wg_ref, wu_ref, wd_ref)
        wg_ref, wu_ref, wd_ref = wg_s, wu_s, wd_s
    side_in_hbm, (o_hbm, os_hbm), side_out_hbm = rest[:n_side], rest[n_side:n_side + 2], rest[n_side + 2:]
    nt = SEQ // TM_FFN
    part = TM_FFN // ROW_PARTS

    def half_step(x, mod):
        if first:
            x = _ln(x, ing_ref[...], inb_ref[...])
        return _ffn_math(x, mod, wg_ref, wu_ref, wd_ref, lng_ref, lnb_ref, layer=layer, sub=sub)

    def prompt_body(x_ref, *tiles):
        side_in, o_ref, side_out = tiles[:n_side], tiles[n_side], tiles[n_side + 1:]
        b = lax.div(pl.program_id(0), nt)
        mod = lambda k: adap_ref[k % 3, pl.ds(b, 1), :]
        for i in range(ROW_PARTS):
            rows = slice(i * part, (i + 1) * part)
            o_ref[rows, :] = half_step(x_ref[rows, :], mod)
        for src, dst in zip(side_in, side_out):
            dst[...] = src[...].astype(BF16)

    def sample_body(x_ref, o_ref):
        o_ref[...] = half_step(x_ref[...], lambda k: adas_ref[k % 3])

    in_specs, out_specs, s_spec = specs
    pltpu.emit_pipeline(prompt_body, grid=(BATCH * nt,), in_specs=in_specs, out_specs=out_specs)(
        x_hbm, *side_in_hbm, o_hbm, *side_out_hbm)
    pltpu.emit_pipeline(sample_body, grid=(xs_hbm.shape[0] // s_spec.block_shape[0],),
                        in_specs=[s_spec], out_specs=[s_spec])(xs_hbm, os_hbm)


def _ffn_both_call(x2d, xs2d, ada, wg, wu, wd, post_g, post_b, in_g, in_b, *, layer, sub, first, side=()):
    n_steps = BATCH * (SEQ // TM_FFN)
    x_spec = pl.BlockSpec((TM_FFN, D_MODEL), lambda s: (s, 0))
    s_spec = pl.BlockSpec((TM_FFN // ROW_PARTS, D_MODEL), lambda s: (s, 0))
    k0 = 0 if sub == 0 else 6
    side_in_specs, side_out_specs, side_shapes = [], [], []
    for w, wl in side:
        k_dim, n_dim = w.shape[1:]
        steps = n_steps
        while (k_dim // steps) % BF16_SUBLANES:
            steps //= 2
        per = n_steps // steps
        side_in_specs.append(pl.BlockSpec((None, k_dim // steps, n_dim),
                                          lambda s, wl=wl, per=per: (wl, s // per, 0)))
        side_out_specs.append(pl.BlockSpec((k_dim // steps, n_dim), lambda s, per=per: (s // per, 0)))
        side_shapes.append(jax.ShapeDtypeStruct((k_dim, n_dim), BF16))
    in_hbm = pl.BlockSpec(memory_space=pl.ANY)
    own_weights = wg.ndim == 3
    kern = functools.partial(_ffn_both_kernel, layer=layer, sub=sub, first=first, n_side=len(side),
                             specs=([x_spec] + side_in_specs, [x_spec] + side_out_specs, s_spec),
                             own_weights=own_weights)
    whole = lambda a: _const_spec(a.shape, (0,) * a.ndim)
    res = pl.pallas_call(
        kern,
        grid=(1,),
        in_specs=[in_hbm, in_hbm,
                  _const_spec((None, 3, BATCH, D_MODEL), (layer, k0 // 3, DEC_BATCH // BATCH, 0)),
                  _const_spec((None, 3, DEC_BATCH, D_MODEL), (layer, k0 // 3, 0, 0))]
                 + [in_hbm if own_weights else whole(w) for w in (wg, wu, wd)]
                 + [whole(a) for a in (post_g, post_b, in_g, in_b)]
                 + [in_hbm] * len(side),
        scratch_shapes=[pltpu.VMEM(w.shape[1:], BF16) for w in (wg, wu, wd)] if own_weights else [],
        out_specs=[in_hbm] * (2 + len(side)),
        out_shape=[jax.ShapeDtypeStruct(x2d.shape, F32), jax.ShapeDtypeStruct(xs2d.shape, F32)] + side_shapes,
        compiler_params=pltpu.CompilerParams(vmem_limit_bytes=VMEM_LIMIT_BYTES),
        name=f"ffn{sub}_{layer}",
    )(x2d, xs2d, ada, ada, wg, wu, wd, post_g, post_b, in_g, in_b, *[w for w, _ in side])
    return res[0], res[1], list(res[2:])


def _pool_lane_consts():
    lane = lax.broadcasted_iota(jnp.int32, (1, G), 1)
    grp = lane // POOL_GDIM
    win = jnp.where(grp == 0, float(POOL_WINDOWS[0]),
                    jnp.where(grp == 1, float(POOL_WINDOWS[1]),
                              jnp.where(grp == 2, float(POOL_WINDOWS[2]), float(POOL_WINDOWS[3]))))
    return grp, win


def _pool_select(grp, s2, s4, s8, s16):
    return jnp.where(grp == 0, s2, jnp.where(grp == 1, s4, jnp.where(grp == 2, s8, s16)))


def _rms_group(y, g, eps=1e-6):
    return y * lax.rsqrt(jnp.mean(y * y, axis=-1, keepdims=True) + eps) * g


def _mixer_front(x, mod, win_ref):
    sh, sc = mod(3), mod(4)
    xm = (x * (1.0 + _rows(x, sc)) + _rows(x, sh)).astype(BF16)
    return _dot(xm, win_ref[...])


def _mixer_norm(ya, yb, yc, yd, onorm_ref, layer):
    parts = [ya, yb, yc, yd]
    normed = [_rms_group(p, onorm_ref[pl.ds(layer, 1), i * G:(i + 1) * G]).astype(BF16)
              for i, p in enumerate(parts)]
    return jnp.concatenate(normed, axis=-1)


def _mixer_back(x, mixn, mod, wout_ref, lng_ref, lnb_ref, layer):
    y = ALPHA * x + (1.0 + _rows(x, mod(5))) * _dot(mixn, wout_ref[...])
    return _ln(y, lng_ref[layer, pl.ds(1, 1), :], lnb_ref[layer, pl.ds(1, 1), :])


def _mixer_prompt_mid(proj, pos0, r0, last, sguw_ref, sgub_ref, sgug_ref, sgulb_ref,
                      poolw_ref, pools_ref, sconvw_ref, dconvw_ref, dconvb_ref, cvg_ref, cvb_ref,
                      onorm_ref, sguv_ref, poolst_ref, sconvst_ref, dconvst_ref,
                      pool_ext, sconv_ext, dconv_ext, *, layer):
    n = proj.shape[0]
    lrow = pl.ds(layer, 1)
    col = lambda i: proj[:, i * G:(i + 1) * G]

    u = _gelu(col(0))
    v = _ln(_gelu(col(1)), sgug_ref[lrow, :], sgulb_ref[lrow, :])
    if last:
        sguv_ref[...] = v[n - CHUNK:, :]
    r_i = lax.broadcasted_iota(jnp.int32, (CHUNK, CHUNK), 0)
    c_i = lax.broadcasted_iota(jnp.int32, (CHUNK, CHUNK), 1)
    wcat = jnp.concatenate([jnp.where(r_i >= c_i, sguw_ref[h], 0.0) for h in range(A_HEADS)],
                           axis=1).astype(BF16)
    head = lax.broadcasted_iota(jnp.int32, (1, G), 1) // HEAD_DIM
    bias = sgub_ref[...]
    mixed = []
    for c in range(n // CHUNK):
        vc = v[c * CHUNK:(c + 1) * CHUNK, :]
        rhs = jnp.concatenate([jnp.where(head == h, vc, 0.0) for h in range(A_HEADS)],
                              axis=0).astype(BF16)
        mixed.append(_dot(wcat, rhs) + bias)
    ya = u * jnp.concatenate(mixed, axis=0)

    xb = col(2)
    pool_ext[r0 + POOL_CARRY:r0 + POOL_CARRY + n, :] = xb
    e = pool_ext[r0:r0 + POOL_CARRY + n, :]
    s2 = e + pltpu.roll(e, 1, 0)
    s4 = s2 + pltpu.roll(s2, 2, 0)
    s8 = s4 + pltpu.roll(s4, 4, 0)
    s16 = s8 + pltpu.roll(s8, 8, 0)
    grp, win = _pool_lane_consts()
    ssel = _pool_select(grp, s2, s4, s8, s16)[POOL_CARRY:, :]
    pos = pos0 + lax.broadcasted_iota(jnp.int32, (n, 1), 0)
    cnt = jnp.minimum(win, (pos + 1).astype(F32))
    pooled = (ssel / cnt - xb).astype(BF16)
    yb = _dot(pooled, poolw_ref[...]) * pools_ref[lrow, :]
    if last:
        poolst_ref[...] = pool_ext[r0 + n + POOL_CARRY - POOL_BUF:r0 + n + POOL_CARRY, :]

    zc = col(4) * col(5)
    sconv_ext[r0 + SCONV_CARRY:r0 + SCONV_CARRY + n, :] = zc
    es = sconv_ext[r0:r0 + SCONV_CARRY + n, :]
    yconv = (sconvw_ref[layer, pl.ds(2, 1), :] * es
             + sconvw_ref[layer, pl.ds(1, 1), :] * pltpu.roll(es, 1, 0)
             + sconvw_ref[layer, pl.ds(0, 1), :] * pltpu.roll(es, 2, 0))
    yc = col(3) * yconv[SCONV_CARRY:, :]
    if last:
        sconvst_ref[...] = sconv_ext[r0 + n + SCONV_CARRY - (SCONV_W - 1):r0 + n + SCONV_CARRY, :]

    zd = col(6) * jax.nn.sigmoid(col(7))
    dconv_ext[r0 + DCONV_CARRY:r0 + DCONV_CARRY + n, :] = zd
    if last:
        dconvst_ref[...] = dconv_ext[r0 + n + DCONV_CARRY - (DCONV_W - 1):r0 + n + DCONV_CARRY, :]
    ed = dconv_ext[r0:r0 + DCONV_CARRY + n, :]
    acc = jnp.zeros((n, G), F32)
    for r in range(SUBLANES):
        er = ed if r == 0 else pltpu.roll(ed, r, 0)
        for q in range(DCONV_CARRY // SUBLANES):
            tap = SUBLANES * q + r
            if tap >= DCONV_W:
                continue
            lo = DCONV_CARRY - SUBLANES * q
            acc = acc + dconvw_ref[layer, pl.ds(DCONV_W - 1 - tap, 1), :] * er[lo:lo + n, :]
    hd = _ln(acc + dconvb_ref[lrow, :], cvg_ref[lrow, :], cvb_ref[lrow, :])
    yd = _silu(hd)

    return _mixer_norm(ya, yb, yc, yd, onorm_ref, layer)


def _mixer_prompt_kernel(x_ref, ada_ref, win_ref, wout_ref, sguw_ref, sgub_ref, sgug_ref, sgulb_ref,
                         poolw_ref, pools_ref, sconvw_ref, dconvw_ref, dconvb_ref, cvg_ref, cvb_ref,
                         onorm_ref, lng_ref, lnb_ref,
                         o_ref, sguv_ref, poolst_ref, sconvst_ref, dconvst_ref,
                         pool_ext, sconv_ext, dconv_ext, *, layer, sub_rows):
    tm = x_ref.shape[0]
    b = pl.program_id(0)
    j = pl.program_id(1)

    @pl.when(j == 0)
    def _():
        pool_ext[0:POOL_CARRY, :] = jnp.zeros((POOL_CARRY, G), F32)
        sconv_ext[0:SCONV_CARRY, :] = jnp.zeros((SCONV_CARRY, G), F32)
        dconv_ext[0:DCONV_CARRY, :] = jnp.zeros((DCONV_CARRY, G), F32)

    @pl.when(j > 0)
    def _():
        pool_ext[0:POOL_CARRY, :] = pool_ext[tm:tm + POOL_CARRY, :]
        sconv_ext[0:SCONV_CARRY, :] = sconv_ext[tm:tm + SCONV_CARRY, :]
        dconv_ext[0:DCONV_CARRY, :] = dconv_ext[tm:tm + DCONV_CARRY, :]

    mod = lambda k: ada_ref[k, pl.ds(b, 1), :]
    n_sub = tm // sub_rows
    rows = lambda h: slice(h * sub_rows, (h + 1) * sub_rows)
    front = lambda h: _mixer_front(x_ref[rows(h), :], mod, win_ref)
    proj = front(0)
    for h in range(n_sub):
        proj_next = front(h + 1) if h + 1 < n_sub else None
        mixn = _mixer_prompt_mid(
            proj, j * tm + h * sub_rows, h * sub_rows, h == n_sub - 1,
            sguw_ref, sgub_ref, sgug_ref, sgulb_ref, poolw_ref, pools_ref, sconvw_ref,
            dconvw_ref, dconvb_ref, cvg_ref, cvb_ref, onorm_ref,
            sguv_ref, poolst_ref, sconvst_ref, dconvst_ref, pool_ext, sconv_ext, dconv_ext, layer=layer)
        o_ref[rows(h), :] = _mixer_back(x_ref[rows(h), :], mixn, mod, wout_ref, lng_ref, lnb_ref, layer)
        proj = proj_next


def _mixer_prompt_call(x2d, ada, w_in, w_out, sgu_w, sgu_bias, small, *, layer):
    (sgu_ln_g, sgu_ln_b, pool_bd, pool_scale, sconv_w, dconv_w, dconv_b, conv_ln_g, conv_ln_b,
     out_norm_g, post_g, post_b) = small
    tm = TM_MIX
    nt = SEQ // tm
    x_spec = pl.BlockSpec((tm, D_MODEL), lambda b, j: (b * nt + j, 0))
    st_spec = lambda n: pl.BlockSpec((None, n, G), lambda b, j: (b, 0, 0))
    full2 = lambda a: _const_spec(a.shape, (0,) * a.ndim)
    return pl.pallas_call(
        functools.partial(_mixer_prompt_kernel, layer=layer, sub_rows=SUB_MIX),
        grid=(BATCH, nt),
        in_specs=[
            x_spec,
            _const_spec((None, N_ADA, BATCH, D_MODEL), (layer, 0, DEC_BATCH // BATCH, 0)),
            _const_spec((D_MODEL, IN_COLS), (0, 0)),
            _const_spec((D_MODEL, D_MODEL), (0, 0)),
            _const_spec((None, A_HEADS, CHUNK, CHUNK), (layer, 0, 0, 0)),
            _const_spec((None, CHUNK, G), (layer, 0, 0)),
            full2(sgu_ln_g), full2(sgu_ln_b),
            _const_spec((None, G, G), (layer, 0, 0)),
            full2(pool_scale), full2(sconv_w), full2(dconv_w), full2(dconv_b),
            full2(conv_ln_g), full2(conv_ln_b), full2(out_norm_g), full2(post_g), full2(post_b),
        ],
        out_specs=[x_spec, st_spec(CHUNK), st_spec(POOL_BUF), st_spec(SCONV_W - 1), st_spec(DCONV_W - 1)],
        out_shape=[
            jax.ShapeDtypeStruct((BATCH * SEQ, D_MODEL), F32),
            jax.ShapeDtypeStruct((BATCH, CHUNK, G), F32),
            jax.ShapeDtypeStruct((BATCH, POOL_BUF, G), F32),
            jax.ShapeDtypeStruct((BATCH, SCONV_W - 1, G), F32),
            jax.ShapeDtypeStruct((BATCH, DCONV_W - 1, G), F32),
        ],
        scratch_shapes=[
            pltpu.VMEM((tm + POOL_CARRY, G), F32),
            pltpu.VMEM((tm + SCONV_CARRY, G), F32),
            pltpu.VMEM((tm + DCONV_CARRY, G), F32),
        ],
        compiler_params=pltpu.CompilerParams(
            dimension_semantics=("arbitrary", "arbitrary"), vmem_limit_bytes=VMEM_LIMIT_BYTES),
        name=f"mixer_p{layer}",
    )(x2d, ada, w_in, w_out, sgu_w, sgu_bias, sgu_ln_g, sgu_ln_b, pool_bd, pool_scale, sconv_w,
      dconv_w, dconv_b, conv_ln_g, conv_ln_b, out_norm_g, post_g, post_b)


def _mixer_sample_kernel(x_ref, ada_ref, win_ref, wout_ref, sguw_ref, sgub_ref, sgug_ref, sgulb_ref,
                         poolw_ref, pools_ref, sconvw_ref, dconvw_ref, dconvb_ref, cvg_ref, cvb_ref,
                         onorm_ref, lng_ref, lnb_ref, poolin_ref, sconvin_ref, dconvin_ref,
                         o_ref, sguv_ref, poolst_ref, sconvst_ref, dconvst_ref, *, layer):
    nb = DEC_BATCH
    lrow = pl.ds(layer, 1)
    x = x_ref[...]
    mod = lambda k: ada_ref[k]
    proj = _mixer_front(x, mod, win_ref)
    col = lambda i: proj[:, i * G:(i + 1) * G]
    slab = lambda a, t: a[t * nb:(t + 1) * nb, :]

    u = _gelu(col(0))
    v = _ln(_gelu(col(1)), sgug_ref[lrow, :], sgulb_ref[lrow, :])
    for t in range(DEC_SEQ):
        sguv_ref[t] = slab(v, t)
    mixed = []
    for t in range(DEC_SEQ):
        m = jnp.zeros((nb, G), F32) + sgub_ref[pl.ds(t, 1), :]
        for s in range(t + 1):
            m = m + sguw_ref[t, pl.ds(s, 1), :] * slab(v, s)
        mixed.append(m)
    ya = u * jnp.concatenate(mixed, axis=0)

    xb = col(2)
    e = [poolin_ref[i] for i in range(POOL_BUF)] + [slab(xb, t) for t in range(DEC_SEQ)]
    n_e = len(e)
    s2 = [e[i] + e[i - 1] if i >= 1 else None for i in range(n_e)]
    s4 = [s2[i] + s2[i - 2] if i >= 3 else None for i in range(n_e)]
    s8 = [s4[i] + s4[i - 4] if i >= 7 else None for i in range(n_e)]
    s16 = [s8[i] + s8[i - 8] if i >= 15 else None for i in range(n_e)]
    grp, win = _pool_lane_consts()
    cnt = jnp.minimum(win, float(PAST_LEN + 1))
    sel = jnp.concatenate([_pool_select(grp, s2[i], s4[i], s8[i], s16[i])
                           for i in range(POOL_BUF, n_e)], axis=0)
    pooled = (sel / cnt - xb).astype(BF16)
    yb = _dot(pooled, poolw_ref[...]) * pools_ref[lrow, :]
    for i in range(POOL_BUF):
        poolst_ref[i] = e[n_e - POOL_BUF + i]

    zc = col(4) * col(5)
    es = [sconvin_ref[i] for i in range(SCONV_W - 1)] + [slab(zc, t) for t in range(DEC_SEQ)]
    yconv = jnp.concatenate(
        [sum(sconvw_ref[layer, pl.ds(k, 1), :] * es[t + k] for k in range(SCONV_W))
         for t in range(DEC_SEQ)], axis=0)
    yc = col(3) * yconv
    for i in range(SCONV_W - 1):
        sconvst_ref[i] = es[len(es) - (SCONV_W - 1) + i]

    zd = col(6) * jax.nn.sigmoid(col(7))
    for i in range(DCONV_W - 1 - DEC_SEQ):
        dconvst_ref[i] = dconvin_ref[i + DEC_SEQ]
    for t in range(DEC_SEQ):
        dconvst_ref[DCONV_W - 1 - DEC_SEQ + t] = slab(zd, t)
    conv = []
    for t in range(DEC_SEQ):
        a = jnp.zeros((nb, G), F32)
        for k in range(DCONV_W):
            i = t + k
            src = dconvin_ref[i] if i < DCONV_W - 1 else slab(zd, i - (DCONV_W - 1))
            a = a + dconvw_ref[layer, pl.ds(k, 1), :] * src
        conv.append(a)
    hd = _ln(jnp.concatenate(conv, axis=0) + dconvb_ref[lrow, :], cvg_ref[lrow, :], cvb_ref[lrow, :])
    yd = _silu(hd)

    mixn = _mixer_norm(ya, yb, yc, yd, onorm_ref, layer)
    o_ref[...] = _mixer_back(x, mixn, mod, wout_ref, lng_ref, lnb_ref, layer)


def _mixer_sample_call(x2d, ada, w_in, w_out, sgu_wsm, sgu_bias, small, pool_in, sconv_in, dconv_in,
                       *, layer):
    (sgu_ln_g, sgu_ln_b, pool_bd, pool_scale, sconv_w, dconv_w, dconv_b, conv_ln_g, conv_ln_b,
     out_norm_g, post_g, post_b) = small
    rows = DEC_SEQ * DEC_BATCH
    full2 = lambda a: _const_spec(a.shape, (0,) * a.ndim)
    st = lambda n: _const_spec((None, n, DEC_BATCH, G), (layer, 0, 0, 0))
    st_out = lambda n: pl.BlockSpec((n, DEC_BATCH, G), lambda i: (0, 0, 0))
    return pl.pallas_call(
        functools.partial(_mixer_sample_kernel, layer=layer),
        grid=(1,),
        in_specs=[
            _const_spec((rows, D_MODEL), (0, 0)),
            _const_spec((None, N_ADA, DEC_BATCH, D_MODEL), (layer, 0, 0, 0)),
            _const_spec((D_MODEL, IN_COLS), (0, 0)),
            _const_spec((D_MODEL, D_MODEL), (0, 0)),
            _const_spec((None, DEC_SEQ, DEC_SEQ, G), (layer, 0, 0, 0)),
            _const_spec((None, DEC_SEQ, G), (layer, 0, 0)),
            full2(sgu_ln_g), full2(sgu_ln_b),
            _const_spec((None, G, G), (layer, 0, 0)),
            full2(pool_scale), full2(sconv_w), full2(dconv_w), full2(dconv_b),
            full2(conv_ln_g), full2(conv_ln_b), full2(out_norm_g), full2(post_g), full2(post_b),
            st(POOL_BUF), st(SCONV_W - 1), st(DCONV_W - 1),
        ],
        out_specs=[pl.BlockSpec((rows, D_MODEL), lambda i: (0, 0)),
                   st_out(DEC_SEQ), st_out(POOL_BUF), st_out(SCONV_W - 1), st_out(DCONV_W - 1)],
        out_shape=[
            jax.ShapeDtypeStruct((rows, D_MODEL), F32),
            jax.ShapeDtypeStruct((DEC_SEQ, DEC_BATCH, G), F32),
            jax.ShapeDtypeStruct((POOL_BUF, DEC_BATCH, G), F32),
            jax.ShapeDtypeStruct((SCONV_W - 1, DEC_BATCH, G), F32),
            jax.ShapeDtypeStruct((DCONV_W - 1, DEC_BATCH, G), F32),
        ],
        compiler_params=pltpu.CompilerParams(
            dimension_semantics=("arbitrary",), vmem_limit_bytes=VMEM_LIMIT_BYTES),
        name=f"mixer_s{layer}",
    )(x2d, ada, w_in, w_out, sgu_wsm, sgu_bias, sgu_ln_g, sgu_ln_b, pool_bd, pool_scale, sconv_w,
      dconv_w, dconv_b, conv_ln_g, conv_ln_b, out_norm_g, post_g, post_b, pool_in, sconv_in, dconv_in)


def kernel(x_prompt, x_sample, state_pool, state_sconv, state_dconv, c_prompt, c_sample, ln_in_g, ln_in_b, w_ada, b_ada, ffn1_w_gate, ffn1_w_up, ffn1_w_down, w_in, sgu_ln_g, sgu_ln_b, sgu_w, sgu_b, pool_w, pool_scale, sconv_w, dconv_w, dconv_b, conv_ln_g, conv_ln_b, out_norm_g, w_out, ffn2_w_gate, ffn2_w_up, ffn2_w_down, post_ln_g, post_ln_b):
    eye = jnp.eye(len(POOL_WINDOWS), dtype=F32)
    pool_bd = jnp.einsum('lgcd,gh->lgchd', pool_w, eye).reshape(DEPTH, G, G).astype(BF16)
    sgu_bias = jnp.repeat(jnp.transpose(sgu_b, (0, 2, 1)), HEAD_DIM, axis=-1)
    sgu_wsm = jnp.repeat(jnp.transpose(sgu_w[:, :, :DEC_SEQ, :DEC_SEQ], (0, 2, 3, 1)),
                         HEAD_DIM, axis=-1)
    in_g, in_b = ln_in_g.reshape(1, D_MODEL), ln_in_b.reshape(1, D_MODEL)
    small = (sgu_ln_g, sgu_ln_b, pool_bd, pool_scale, sconv_w, dconv_w, dconv_b, conv_ln_g,
             conv_ln_b, out_norm_g, post_ln_g, post_ln_b)

    ada = _ada_call(jnp.concatenate([c_sample, c_prompt], axis=0), w_ada, b_ada)

    xs = jnp.transpose(x_sample, (1, 0, 2)).reshape(DEC_SEQ * DEC_BATCH, D_MODEL)
    pool_in = jnp.transpose(state_pool, (0, 2, 1, 3))
    sconv_in = jnp.transpose(state_sconv, (0, 2, 1, 3))
    dconv_in = jnp.transpose(state_dconv, (0, 2, 1, 3))
    xp = x_prompt.reshape(BATCH * SEQ, D_MODEL)

    ffn1 = [ffn1_w_gate, ffn1_w_up, ffn1_w_down]
    ffn = functools.partial(_ffn_both_call, post_g=post_ln_g, post_b=post_ln_b, in_g=in_g, in_b=in_b)
    outs_p, outs_s = [], []
    for l in range(DEPTH):
        last = l == DEPTH - 1
        side = [(w_in, l), (w_out, l), (ffn2_w_gate, l), (ffn2_w_up, l), (ffn2_w_down, l)]
        xp, xs, (w_in_b, w_out_b, *ffn2) = ffn(xp, xs, ada, *ffn1, layer=l, sub=0, first=(l == 0), side=side)
        xp, *st_p = _mixer_prompt_call(xp, ada, w_in_b, w_out_b, sgu_w, sgu_bias, small, layer=l)
        xs, *st_s = _mixer_sample_call(xs, ada, w_in_b, w_out_b, sgu_wsm, sgu_bias, small,
                                       pool_in, sconv_in, dconv_in, layer=l)
        side = [] if last else [(ffn1_w_gate, l + 1), (ffn1_w_up, l + 1), (ffn1_w_down, l + 1)]
        xp, xs, ffn1 = ffn(xp, xs, ada, *ffn2, layer=l, sub=2, first=False, side=side)
        outs_p.append(st_p)
        outs_s.append(st_s)

    y_prompt = xp.reshape(BATCH, SEQ, D_MODEL)
    y_sample = jnp.transpose(xs.reshape(DEC_SEQ, DEC_BATCH, D_MODEL), (1, 0, 2))
    stack_p = lambda i: jnp.stack([o[i] for o in outs_p], axis=0)
    stack_s = lambda i: jnp.transpose(jnp.stack([o[i] for o in outs_s], axis=0), (0, 2, 1, 3))
    return (y_prompt, y_sample, stack_p(0), stack_s(0), stack_p(1), stack_s(1),
            stack_p(2), stack_s(2), stack_p(3), stack_s(3))
```

```python
import functools

import jax
import jax.numpy as jnp
from jax import lax
from jax.experimental import pallas as pl
from jax.experimental.pallas import tpu as pltpu

D_MODEL = 1024
BATCH = 8
SEQ = 2048
DEPTH = 2
DEC_BATCH = 128
DEC_SEQ = 8
PAST_LEN = 16384
G = 256
HEAD_DIM = 64
A_HEADS = G // HEAD_DIM
CHUNK = 128
POOL_WINDOWS = (2, 4, 8, 16)
POOL_GDIM = G // len(POOL_WINDOWS)
POOL_BUF = 15
SCONV_W = 3
DCONV_W = 31
D_FF = 2816
N_ADA = 9
ALPHA = (2 * DEPTH) ** 0.25
IN_COLS = 8 * G
N_SEQ_ALL = DEC_BATCH + BATCH
ADA_PER_STEP = 3

SUBLANES = 8
BF16_SUBLANES = 16
POOL_CARRY = 16
SCONV_CARRY = 16
DCONV_CARRY = 32
VMEM_LIMIT_BYTES = 56 * 1024 * 1024

TM_FFN = 1024
ROW_PARTS = 2
TM_MIX = 1024
SUB_MIX = 512
W_CHUNKS = 8

BF16 = jnp.bfloat16
F32 = jnp.float32


def _ln(x, g, b, eps=1e-5):
    mu = jnp.mean(x, axis=-1, keepdims=True)
    xc = x - mu
    var = jnp.mean(xc * xc, axis=-1, keepdims=True)
    return xc * lax.rsqrt(var + eps) * g + b


def _gelu(x):
    return 0.5 * x * (1.0 + lax.erf(x * 0.7071067811865476))


def _silu(x):
    return x * jax.nn.sigmoid(x)


def _rows(x, m):
    if m.shape[0] == 1 or m.shape[0] == x.shape[0]:
        return m
    reps = x.shape[0] // m.shape[0]
    return jnp.broadcast_to(m[None], (reps,) + m.shape).reshape(x.shape)


def _dot(a, b):
    return jnp.dot(a, b, preferred_element_type=F32)


def _const_spec(shape, index):
    return pl.BlockSpec(shape, lambda *_: index, pipeline_mode=pl.Buffered(1))


def _ada_kernel(c_ref, w_ref, b_ref, o_ref):
    s = _silu(c_ref[...]).astype(BF16)
    y = _dot(s, w_ref[...].astype(BF16))
    for k in range(ADA_PER_STEP):
        cols = slice(k * D_MODEL, (k + 1) * D_MODEL)
        o_ref[k] = y[:, cols] + b_ref[pl.ds(pl.program_id(0), 1), cols]


def _ada_call(c_all, w_ada, b_ada):
    return pl.pallas_call(
        _ada_kernel,
        grid=(DEPTH, N_ADA // ADA_PER_STEP),
        in_specs=[
            pl.BlockSpec((N_SEQ_ALL, D_MODEL), lambda l, k: (0, 0)),
            pl.BlockSpec((None, D_MODEL, ADA_PER_STEP * D_MODEL), lambda l, k: (l, 0, k)),
            pl.BlockSpec((DEPTH, ADA_PER_STEP * D_MODEL), lambda l, k: (0, k)),
        ],
        out_specs=pl.BlockSpec((None, ADA_PER_STEP, N_SEQ_ALL, D_MODEL), lambda l, k: (l, k, 0, 0)),
        out_shape=jax.ShapeDtypeStruct((DEPTH, N_ADA, N_SEQ_ALL, D_MODEL), F32),
        compiler_params=pltpu.CompilerParams(
            dimension_semantics=("arbitrary", "arbitrary"), vmem_limit_bytes=VMEM_LIMIT_BYTES),
        name="ada",
    )(c_all, w_ada, b_ada)


def _ffn_math(x, mod, wg_ref, wu_ref, wd_ref, lng_ref, lnb_ref, *, layer, sub):
    k0 = 0 if sub == 0 else 6
    sh, sc, gt = mod(k0), mod(k0 + 1), mod(k0 + 2)
    xm = (x * (1.0 + _rows(x, sc)) + _rows(x, sh)).astype(BF16)
    hg = _dot(xm, wg_ref[...])
    hu = _dot(xm, wu_ref[...])
    h = (_silu(hg) * hu).astype(BF16)
    gain = 0.5 * (1.0 + _rows(x, gt))
    half = x.shape[0] // 2
    outs = []
    for r in (slice(0, half), slice(half, 2 * half)):
        y = ALPHA * x[r] + (gain if gain.shape[0] == 1 else gain[r]) * _dot(h[r], wd_ref[...])
        outs.append(_ln(y, lng_ref[layer, pl.ds(sub, 1), :], lnb_ref[layer, pl.ds(sub, 1), :]))
    return jnp.concatenate(outs, axis=0)


def _ffn_both_kernel(x_hbm, xs_hbm, adap_ref, adas_ref, wg_ref, wu_ref, wd_ref, lng_ref, lnb_ref,
                     ing_ref, inb_ref, *rest, layer, sub, first, n_side, specs, own_weights):
    if own_weights:
        *rest, wg_s, wu_s, wd_s = rest

        def to_bf16(*chunks):
            c = pl.program_id(0)
            for src, dst in zip(chunks, (wg_s, wu_s, wd_s)):
                n = src.shape[0]
                dst[pl.ds(pl.multiple_of(c * n, n), n), :] = src[...].astype(BF16)

        chunk = lambda w: pl.BlockSpec((None, w.shape[1] // W_CHUNKS, w.shape[2]), lambda c: (layer, c, 0))
        pltpu.emit_pipeline(to_bf16, grid=(W_CHUNKS,), in_specs=[chunk(wg_ref), chunk(wu_ref), chunk(wd_ref)],
                            out_specs=[])(wg_ref, wu_ref, wd_ref)
        wg_ref, wu_ref, wd_ref = wg_s, wu_s, wd_s
    side_in_hbm, (o_hbm, os_hbm), side_out_hbm = rest[:n_side], rest[n_side:n_side + 2], rest[n_side + 2:]
    nt = SEQ // TM_FFN
    part = TM_FFN // ROW_PARTS

    def half_step(x, mod):
        if first:
            x = _ln(x, ing_ref[...], inb_ref[...])
        return _ffn_math(x, mod, wg_ref, wu_ref, wd_ref, lng_ref, lnb_ref, layer=layer, sub=sub)

    def prompt_body(x_ref, *tiles):
        side_in, o_ref, side_out = tiles[:n_side], tiles[n_side], tiles[n_side + 1:]
        b = lax.div(pl.program_id(0), nt)
        mod = lambda k: adap_ref[k % 3, pl.ds(b, 1), :]
        for i in range(ROW_PARTS):
            rows = slice(i * part, (i + 1) * part)
            o_ref[rows, :] = half_step(x_ref[rows, :], mod)
        for src, dst in zip(side_in, side_out):
            dst[...] = src[...].astype(BF16)

    def sample_body(x_ref, o_ref):
        for i in range(x_ref.shape[0] // part):
            rows = slice(i * part, (i + 1) * part)
            o_ref[rows, :] = half_step(x_ref[rows, :], lambda k: adas_ref[k % 3])

    in_specs, out_specs, s_spec = specs
    pltpu.emit_pipeline(prompt_body, grid=(BATCH * nt,), in_specs=in_specs, out_specs=out_specs)(
        x_hbm, *side_in_hbm, o_hbm, *side_out_hbm)
    pltpu.emit_pipeline(sample_body, grid=(xs_hbm.shape[0] // s_spec.block_shape[0],),
                        in_specs=[s_spec], out_specs=[s_spec])(xs_hbm, os_hbm)


def _ffn_both_call(x2d, xs2d, ada, wg, wu, wd, post_g, post_b, in_g, in_b, *, layer, sub, first, side=()):
    n_steps = BATCH * (SEQ // TM_FFN)
    x_spec = pl.BlockSpec((TM_FFN, D_MODEL), lambda s: (s, 0))
    s_spec = pl.BlockSpec((TM_FFN, D_MODEL), lambda s: (s, 0))
    k0 = 0 if sub == 0 else 6
    side_in_specs, side_out_specs, side_shapes = [], [], []
    for w, wl in side:
        k_dim, n_dim = w.shape[1:]
        steps = n_steps
        while (k_dim // steps) % BF16_SUBLANES:
            steps //= 2
        per = n_steps // steps
        side_in_specs.append(pl.BlockSpec((None, k_dim // steps, n_dim),
                                          lambda s, wl=wl, per=per: (wl, s // per, 0)))
        side_out_specs.append(pl.BlockSpec((k_dim // steps, n_dim), lambda s, per=per: (s // per, 0)))
        side_shapes.append(jax.ShapeDtypeStruct((k_dim, n_dim), BF16))
    in_hbm = pl.BlockSpec(memory_space=pl.ANY)
    own_weights = wg.ndim == 3
    kern = functools.partial(_ffn_both_kernel, layer=layer, sub=sub, first=first, n_side=len(side),
                             specs=([x_spec] + side_in_specs, [x_spec] + side_out_specs, s_spec),
                             own_weights=own_weights)
    whole = lambda a: _const_spec(a.shape, (0,) * a.ndim)
    res = pl.pallas_call(
        kern,
        grid=(1,),
        in_specs=[in_hbm, in_hbm,
                  _const_spec((None, 3, BATCH, D_MODEL), (layer, k0 // 3, DEC_BATCH // BATCH, 0)),
                  _const_spec((None, 3, DEC_BATCH, D_MODEL), (layer, k0 // 3, 0, 0))]
                 + [in_hbm if own_weights else whole(w) for w in (wg, wu, wd)]
                 + [whole(a) for a in (post_g, post_b, in_g, in_b)]
                 + [in_hbm] * len(side),
        scratch_shapes=[pltpu.VMEM(w.shape[1:], BF16) for w in (wg, wu, wd)] if own_weights else [],
        out_specs=[in_hbm] * (2 + len(side)),
        out_shape=[jax.ShapeDtypeStruct(x2d.shape, F32), jax.ShapeDtypeStruct(xs2d.shape, F32)] + side_shapes,
        compiler_params=pltpu.CompilerParams(vmem_limit_bytes=VMEM_LIMIT_BYTES),
        name=f"ffn{sub}_{layer}",
    )(x2d, xs2d, ada, ada, wg, wu, wd, post_g, post_b, in_g, in_b, *[w for w, _ in side])
    return res[0], res[1], list(res[2:])


def _pool_lane_consts():
    lane = lax.broadcasted_iota(jnp.int32, (1, G), 1)
    grp = lane // POOL_GDIM
    win = jnp.where(grp == 0, float(POOL_WINDOWS[0]),
                    jnp.where(grp == 1, float(POOL_WINDOWS[1]),
                              jnp.where(grp == 2, float(POOL_WINDOWS[2]), float(POOL_WINDOWS[3]))))
    return grp, win


def _pool_select(grp, s2, s4, s8, s16):
    return jnp.where(grp == 0, s2, jnp.where(grp == 1, s4, jnp.where(grp == 2, s8, s16)))


def _rms_group(y, g, eps=1e-6):
    return y * lax.rsqrt(jnp.mean(y * y, axis=-1, keepdims=True) + eps) * g


def _mixer_front(x, mod, win_ref):
    sh, sc = mod(3), mod(4)
    xm = (x * (1.0 + _rows(x, sc)) + _rows(x, sh)).astype(BF16)
    return _dot(xm, win_ref[...])


def _mixer_norm(ya, yb, yc, yd, onorm_ref, layer):
    parts = [ya, yb, yc, yd]
    normed = [_rms_group(p, onorm_ref[pl.ds(layer, 1), i * G:(i + 1) * G]).astype(BF16)
              for i, p in enumerate(parts)]
    return jnp.concatenate(normed, axis=-1)


def _mixer_back(x, mixn, mod, wout_ref, lng_ref, lnb_ref, layer):
    y = ALPHA * x + (1.0 + _rows(x, mod(5))) * _dot(mixn, wout_ref[...])
    return _ln(y, lng_ref[layer, pl.ds(1, 1), :], lnb_ref[layer, pl.ds(1, 1), :])


def _mixer_prompt_mid(proj, pos0, r0, last, sguw_ref, sgub_ref, sgug_ref, sgulb_ref,
                      poolw_ref, pools_ref, sconvw_ref, dconvw_ref, dconvb_ref, cvg_ref, cvb_ref,
                      onorm_ref, sguv_ref, poolst_ref, sconvst_ref, dconvst_ref,
                      pool_ext, sconv_ext, dconv_ext, *, layer):
    n = proj.shape[0]
    lrow = pl.ds(layer, 1)
    col = lambda i: proj[:, i * G:(i + 1) * G]

    u = _gelu(col(0))
    v = _ln(_gelu(col(1)), sgug_ref[lrow, :], sgulb_ref[lrow, :])
    if last:
        sguv_ref[...] = v[n - CHUNK:, :]
    r_i = lax.broadcasted_iota(jnp.int32, (CHUNK, CHUNK), 0)
    c_i = lax.broadcasted_iota(jnp.int32, (CHUNK, CHUNK), 1)
    wcat = jnp.concatenate([jnp.where(r_i >= c_i, sguw_ref[h], 0.0) for h in range(A_HEADS)],
                           axis=1).astype(BF16)
    head = lax.broadcasted_iota(jnp.int32, (1, G), 1) // HEAD_DIM
    bias = sgub_ref[...]
    mixed = []
    for c in range(n // CHUNK):
        vc = v[c * CHUNK:(c + 1) * CHUNK, :]
        rhs = jnp.concatenate([jnp.where(head == h, vc, 0.0) for h in range(A_HEADS)],
                              axis=0).astype(BF16)
        mixed.append(_dot(wcat, rhs) + bias)
    ya = u * jnp.concatenate(mixed, axis=0)

    xb = col(2)
    pool_ext[r0 + POOL_CARRY:r0 + POOL_CARRY + n, :] = xb
    e = pool_ext[r0:r0 + POOL_CARRY + n, :]
    s2 = e + pltpu.roll(e, 1, 0)
    s4 = s2 + pltpu.roll(s2, 2, 0)
    s8 = s4 + pltpu.roll(s4, 4, 0)
    s16 = s8 + pltpu.roll(s8, 8, 0)
    grp, win = _pool_lane_consts()
    ssel = _pool_select(grp, s2, s4, s8, s16)[POOL_CARRY:, :]
    pos = pos0 + lax.broadcasted_iota(jnp.int32, (n, 1), 0)
    cnt = jnp.minimum(win, (pos + 1).astype(F32))
    pooled = (ssel / cnt - xb).astype(BF16)
    yb = _dot(pooled, poolw_ref[...]) * pools_ref[lrow, :]
    if last:
        poolst_ref[...] = pool_ext[r0 + n + POOL_CARRY - POOL_BUF:r0 + n + POOL_CARRY, :]

    zc = col(4) * col(5)
    sconv_ext[r0 + SCONV_CARRY:r0 + SCONV_CARRY + n, :] = zc
    es = sconv_ext[r0:r0 + SCONV_CARRY + n, :]
    yconv = (sconvw_ref[layer, pl.ds(2, 1), :] * es
             + sconvw_ref[layer, pl.ds(1, 1), :] * pltpu.roll(es, 1, 0)
             + sconvw_ref[layer, pl.ds(0, 1), :] * pltpu.roll(es, 2, 0))
    yc = col(3) * yconv[SCONV_CARRY:, :]
    if last:
        sconvst_ref[...] = sconv_ext[r0 + n + SCONV_CARRY - (SCONV_W - 1):r0 + n + SCONV_CARRY, :]

    zd = col(6) * jax.nn.sigmoid(col(7))
    dconv_ext[r0 + DCONV_CARRY:r0 + DCONV_CARRY + n, :] = zd
    if last:
        dconvst_ref[...] = dconv_ext[r0 + n + DCONV_CARRY - (DCONV_W - 1):r0 + n + DCONV_CARRY, :]
    ed = dconv_ext[r0:r0 + DCONV_CARRY + n, :]
    acc = jnp.zeros((n, G), F32)
    for r in range(SUBLANES):
        er = ed if r == 0 else pltpu.roll(ed, r, 0)
        for q in range(DCONV_CARRY // SUBLANES):
            tap = SUBLANES * q + r
            if tap >= DCONV_W:
                continue
            lo = DCONV_CARRY - SUBLANES * q
            acc = acc + dconvw_ref[layer, pl.ds(DCONV_W - 1 - tap, 1), :] * er[lo:lo + n, :]
    hd = _ln(acc + dconvb_ref[lrow, :], cvg_ref[lrow, :], cvb_ref[lrow, :])
    yd = _silu(hd)

    return _mixer_norm(ya, yb, yc, yd, onorm_ref, layer)


def _mixer_prompt_kernel(x_ref, ada_ref, win_ref, wout_ref, sguw_ref, sgub_ref, sgug_ref, sgulb_ref,
                         poolw_ref, pools_ref, sconvw_ref, dconvw_ref, dconvb_ref, cvg_ref, cvb_ref,
                         onorm_ref, lng_ref, lnb_ref,
                         o_ref, sguv_ref, poolst_ref, sconvst_ref, dconvst_ref,
                         pool_ext, sconv_ext, dconv_ext, *, layer, sub_rows):
    tm = x_ref.shape[0]
    b = pl.program_id(0)
    j = pl.program_id(1)

    @pl.when(j == 0)
    def _():
        pool_ext[0:POOL_CARRY, :] = jnp.zeros((POOL_CARRY, G), F32)
        sconv_ext[0:SCONV_CARRY, :] = jnp.zeros((SCONV_CARRY, G), F32)
        dconv_ext[0:DCONV_CARRY, :] = jnp.zeros((DCONV_CARRY, G), F32)

    @pl.when(j > 0)
    def _():
        pool_ext[0:POOL_CARRY, :] = pool_ext[tm:tm + POOL_CARRY, :]
        sconv_ext[0:SCONV_CARRY, :] = sconv_ext[tm:tm + SCONV_CARRY, :]
        dconv_ext[0:DCONV_CARRY, :] = dconv_ext[tm:tm + DCONV_CARRY, :]

    mod = lambda k: ada_ref[k, pl.ds(b, 1), :]
    n_sub = tm // sub_rows
    rows = lambda h: slice(h * sub_rows, (h + 1) * sub_rows)
    front = lambda h: _mixer_front(x_ref[rows(h), :], mod, win_ref)
    proj = front(0)
    for h in range(n_sub):
        proj_next = front(h + 1) if h + 1 < n_sub else None
        mixn = _mixer_prompt_mid(
            proj, j * tm + h * sub_rows, h * sub_rows, h == n_sub - 1,
            sguw_ref, sgub_ref, sgug_ref, sgulb_ref, poolw_ref, pools_ref, sconvw_ref,
            dconvw_ref, dconvb_ref, cvg_ref, cvb_ref, onorm_ref,
            sguv_ref, poolst_ref, sconvst_ref, dconvst_ref, pool_ext, sconv_ext, dconv_ext, layer=layer)
        o_ref[rows(h), :] = _mixer_back(x_ref[rows(h), :], mixn, mod, wout_ref, lng_ref, lnb_ref, layer)
        proj = proj_next


def _mixer_prompt_call(x2d, ada, w_in, w_out, sgu_w, sgu_bias, small, *, layer):
    (sgu_ln_g, sgu_ln_b, pool_bd, pool_scale, sconv_w, dconv_w, dconv_b, conv_ln_g, conv_ln_b,
     out_norm_g, post_g, post_b) = small
    tm = TM_MIX
    nt = SEQ // tm
    x_spec = pl.BlockSpec((tm, D_MODEL), lambda b, j: (b * nt + j, 0))
    st_spec = lambda n: pl.BlockSpec((None, n, G), lambda b, j: (b, 0, 0))
    full2 = lambda a: _const_spec(a.shape, (0,) * a.ndim)
    return pl.pallas_call(
        functools.partial(_mixer_prompt_kernel, layer=layer, sub_rows=SUB_MIX),
        grid=(BATCH, nt),
        in_specs=[
            x_spec,
            _const_spec((None, N_ADA, BATCH, D_MODEL), (layer, 0, DEC_BATCH // BATCH, 0)),
            _const_spec((D_MODEL, IN_COLS), (0, 0)),
            _const_spec((D_MODEL, D_MODEL), (0, 0)),
            _const_spec((None, A_HEADS, CHUNK, CHUNK), (layer, 0, 0, 0)),
            _const_spec((None, CHUNK, G), (layer, 0, 0)),
            full2(sgu_ln_g), full2(sgu_ln_b),
            _const_spec((None, G, G), (layer, 0, 0)),
            full2(pool_scale), full2(sconv_w), full2(dconv_w), full2(dconv_b),
            full2(conv_ln_g), full2(conv_ln_b), full2(out_norm_g), full2(post_g), full2(post_b),
        ],
        out_specs=[x_spec, st_spec(CHUNK), st_spec(POOL_BUF), st_spec(SCONV_W - 1), st_spec(DCONV_W - 1)],
        out_shape=[
            jax.ShapeDtypeStruct((BATCH * SEQ, D_MODEL), F32),
            jax.ShapeDtypeStruct((BATCH, CHUNK, G), F32),
            jax.ShapeDtypeStruct((BATCH, POOL_BUF, G), F32),
            jax.ShapeDtypeStruct((BATCH, SCONV_W - 1, G), F32),
            jax.ShapeDtypeStruct((BATCH, DCONV_W - 1, G), F32),
        ],
        scratch_shapes=[
            pltpu.VMEM((tm + POOL_CARRY, G), F32),
            pltpu.VMEM((tm + SCONV_CARRY, G), F32),
            pltpu.VMEM((tm + DCONV_CARRY, G), F32),
        ],
        compiler_params=pltpu.CompilerParams(
            dimension_semantics=("arbitrary", "arbitrary"), vmem_limit_bytes=VMEM_LIMIT_BYTES),
        name=f"mixer_p{layer}",
    )(x2d, ada, w_in, w_out, sgu_w, sgu_bias, sgu_ln_g, sgu_ln_b, pool_bd, pool_scale, sconv_w,
      dconv_w, dconv_b, conv_ln_g, conv_ln_b, out_norm_g, post_g, post_b)


def _mixer_sample_kernel(x_ref, ada_ref, win_ref, wout_ref, sguw_ref, sgub_ref, sgug_ref, sgulb_ref,
                         poolw_ref, pools_ref, sconvw_ref, dconvw_ref, dconvb_ref, cvg_ref, cvb_ref,
                         onorm_ref, lng_ref, lnb_ref, poolin_ref, sconvin_ref, dconvin_ref,
                         o_ref, sguv_ref, poolst_ref, sconvst_ref, dconvst_ref, *, layer):
    nb = DEC_BATCH
    lrow = pl.ds(layer, 1)
    x = x_ref[...]
    mod = lambda k: ada_ref[k]
    proj = _mixer_front(x, mod, win_ref)
    col = lambda i: proj[:, i * G:(i + 1) * G]
    slab = lambda a, t: a[t * nb:(t + 1) * nb, :]

    u = _gelu(col(0))
    v = _ln(_gelu(col(1)), sgug_ref[lrow, :], sgulb_ref[lrow, :])
    for t in range(DEC_SEQ):
        sguv_ref[t] = slab(v, t)
    mixed = []
    for t in range(DEC_SEQ):
        m = jnp.zeros((nb, G), F32) + sgub_ref[pl.ds(t, 1), :]
        for s in range(t + 1):
            m = m + sguw_ref[t, pl.ds(s, 1), :] * slab(v, s)
        mixed.append(m)
    ya = u * jnp.concatenate(mixed, axis=0)

    xb = col(2)
    e = [poolin_ref[i] for i in range(POOL_BUF)] + [slab(xb, t) for t in range(DEC_SEQ)]
    n_e = len(e)
    s2 = [e[i] + e[i - 1] if i >= 1 else None for i in range(n_e)]
    s4 = [s2[i] + s2[i - 2] if i >= 3 else None for i in range(n_e)]
    s8 = [s4[i] + s4[i - 4] if i >= 7 else None for i in range(n_e)]
    s16 = [s8[i] + s8[i - 8] if i >= 15 else None for i in range(n_e)]
    grp, win = _pool_lane_consts()
    cnt = jnp.minimum(win, float(PAST_LEN + 1))
    sel = jnp.concatenate([_pool_select(grp, s2[i], s4[i], s8[i], s16[i])
                           for i in range(POOL_BUF, n_e)], axis=0)
    pooled = (sel / cnt - xb).astype(BF16)
    yb = _dot(pooled, poolw_ref[...]) * pools_ref[lrow, :]
    for i in range(POOL_BUF):
        poolst_ref[i] = e[n_e - POOL_BUF + i]

    zc = col(4) * col(5)
    es = [sconvin_ref[i] for i in range(SCONV_W - 1)] + [slab(zc, t) for t in range(DEC_SEQ)]
    yconv = jnp.concatenate(
        [sum(sconvw_ref[layer, pl.ds(k, 1), :] * es[t + k] for k in range(SCONV_W))
         for t in range(DEC_SEQ)], axis=0)
    yc = col(3) * yconv
    for i in range(SCONV_W - 1):
        sconvst_ref[i] = es[len(es) - (SCONV_W - 1) + i]

    zd = col(6) * jax.nn.sigmoid(col(7))
    for i in range(DCONV_W - 1 - DEC_SEQ):
        dconvst_ref[i] = dconvin_ref[i + DEC_SEQ]
    for t in range(DEC_SEQ):
        dconvst_ref[DCONV_W - 1 - DEC_SEQ + t] = slab(zd, t)
    conv = []
    for t in range(DEC_SEQ):
        a = jnp.zeros((nb, G), F32)
        for k in range(DCONV_W):
            i = t + k
            src = dconvin_ref[i] if i < DCONV_W - 1 else slab(zd, i - (DCONV_W - 1))
            a = a + dconvw_ref[layer, pl.ds(k, 1), :] * src
        conv.append(a)
    hd = _ln(jnp.concatenate(conv, axis=0) + dconvb_ref[lrow, :], cvg_ref[lrow, :], cvb_ref[lrow, :])
    yd = _silu(hd)

    mixn = _mixer_norm(ya, yb, yc, yd, onorm_ref, layer)
    o_ref[...] = _mixer_back(x, mixn, mod, wout_ref, lng_ref, lnb_ref, layer)


def _mixer_sample_call(x2d, ada, w_in, w_out, sgu_wsm, sgu_bias, small, pool_in, sconv_in, dconv_in,
                       *, layer):
    (sgu_ln_g, sgu_ln_b, pool_bd, pool_scale, sconv_w, dconv_w, dconv_b, conv_ln_g, conv_ln_b,
     out_norm_g, post_g, post_b) = small
    rows = DEC_SEQ * DEC_BATCH
    full2 = lambda a: _const_spec(a.shape, (0,) * a.ndim)
    st = lambda n: _const_spec((None, n, DEC_BATCH, G), (layer, 0, 0, 0))
    st_out = lambda n: pl.BlockSpec((n, DEC_BATCH, G), lambda i: (0, 0, 0))
    return pl.pallas_call(
        functools.partial(_mixer_sample_kernel, layer=layer),
        grid=(1,),
        in_specs=[
            _const_spec((rows, D_MODEL), (0, 0)),
            _const_spec((None, N_ADA, DEC_BATCH, D_MODEL), (layer, 0, 0, 0)),
            _const_spec((D_MODEL, IN_COLS), (0, 0)),
            _const_spec((D_MODEL, D_MODEL), (0, 0)),
            _const_spec((None, DEC_SEQ, DEC_SEQ, G), (layer, 0, 0, 0)),
            _const_spec((None, DEC_SEQ, G), (layer, 0, 0)),
            full2(sgu_ln_g), full2(sgu_ln_b),
            _const_spec((None, G, G), (layer, 0, 0)),
            full2(pool_scale), full2(sconv_w), full2(dconv_w), full2(dconv_b),
            full2(conv_ln_g), full2(conv_ln_b), full2(out_norm_g), full2(post_g), full2(post_b),
            st(POOL_BUF), st(SCONV_W - 1), st(DCONV_W - 1),
        ],
        out_specs=[pl.BlockSpec((rows, D_MODEL), lambda i: (0, 0)),
                   st_out(DEC_SEQ), st_out(POOL_BUF), st_out(SCONV_W - 1), st_out(DCONV_W - 1)],
        out_shape=[
            jax.ShapeDtypeStruct((rows, D_MODEL), F32),
            jax.ShapeDtypeStruct((DEC_SEQ, DEC_BATCH, G), F32),
            jax.ShapeDtypeStruct((POOL_BUF, DEC_BATCH, G), F32),
            jax.ShapeDtypeStruct((SCONV_W - 1, DEC_BATCH, G), F32),
            jax.ShapeDtypeStruct((DCONV_W - 1, DEC_BATCH, G), F32),
        ],
        compiler_params=pltpu.CompilerParams(
            dimension_semantics=("arbitrary",), vmem_limit_bytes=VMEM_LIMIT_BYTES),
        name=f"mixer_s{layer}",
    )(x2d, ada, w_in, w_out, sgu_wsm, sgu_bias, sgu_ln_g, sgu_ln_b, pool_bd, pool_scale, sconv_w,
      dconv_w, dconv_b, conv_ln_g, conv_ln_b, out_norm_g, post_g, post_b, pool_in, sconv_in, dconv_in)


def kernel(x_prompt, x_sample, state_pool, state_sconv, state_dconv, c_prompt, c_sample, ln_in_g, ln_in_b, w_ada, b_ada, ffn1_w_gate, ffn1_w_up, ffn1_w_down, w_in, sgu_ln_g, sgu_ln_b, sgu_w, sgu_b, pool_w, pool_scale, sconv_w, dconv_w, dconv_b, conv_ln_g, conv_ln_b, out_norm_g, w_out, ffn2_w_gate, ffn2_w_up, ffn2_w_down, post_ln_g, post_ln_b):
    eye = jnp.eye(len(POOL_WINDOWS), dtype=F32)
    pool_bd = jnp.einsum('lgcd,gh->lgchd', pool_w, eye).reshape(DEPTH, G, G).astype(BF16)
    sgu_bias = jnp.repeat(jnp.transpose(sgu_b, (0, 2, 1)), HEAD_DIM, axis=-1)
    sgu_wsm = jnp.repeat(jnp.transpose(sgu_w[:, :, :DEC_SEQ, :DEC_SEQ], (0, 2, 3, 1)),
                         HEAD_DIM, axis=-1)
    in_g, in_b = ln_in_g.reshape(1, D_MODEL), ln_in_b.reshape(1, D_MODEL)
    small = (sgu_ln_g, sgu_ln_b, pool_bd, pool_scale, sconv_w, dconv_w, dconv_b, conv_ln_g,
             conv_ln_b, out_norm_g, post_ln_g, post_ln_b)

    ada = _ada_call(jnp.concatenate([c_sample, c_prompt], axis=0), w_ada, b_ada)

    xs = jnp.transpose(x_sample, (1, 0, 2)).reshape(DEC_SEQ * DEC_BATCH, D_MODEL)
    pool_in = jnp.transpose(state_pool, (0, 2, 1, 3))
    sconv_in = jnp.transpose(state_sconv, (0, 2, 1, 3))
    dconv_in = jnp.transpose(state_dconv, (0, 2, 1, 3))
    xp = x_prompt.reshape(BATCH * SEQ, D_MODEL)

    ffn1 = [ffn1_w_gate, ffn1_w_up, ffn1_w_down]
    ffn = functools.partial(_ffn_both_call, post_g=post_ln_g, post_b=post_ln_b, in_g=in_g, in_b=in_b)
    outs_p, outs_s = [], []
    for l in range(DEPTH):
        last = l == DEPTH - 1
        side = [(w_in, l), (w_out, l), (ffn2_w_gate, l), (ffn2_w_up, l), (ffn2_w_down, l)]
        xp, xs, (w_in_b, w_out_b, *ffn2) = ffn(xp, xs, ada, *ffn1, layer=l, sub=0, first=(l == 0), side=side)
        xp, *st_p = _mixer_prompt_call(xp, ada, w_in_b, w_out_b, sgu_w, sgu_bias, small, layer=l)
        xs, *st_s = _mixer_sample_call(xs, ada, w_in_b, w_out_b, sgu_wsm, sgu_bias, small,
                                       pool_in, sconv_in, dconv_in, layer=l)
        side = [] if last else [(ffn1_w_gate, l + 1), (ffn1_w_up, l + 1), (ffn1_w_down, l + 1)]
        xp, xs, ffn1 = ffn(xp, xs, ada, *ffn2, layer=l, sub=2, first=False, side=side)
        outs_p.append(st_p)
        outs_s.append(st_s)

    y_prompt = xp.reshape(BATCH, SEQ, D_MODEL)
    y_sample = jnp.transpose(xs.reshape(DEC_SEQ, DEC_BATCH, D_MODEL), (1, 0, 2))
    stack_p = lambda i: jnp.stack([o[i] for o in outs_p], axis=0)
    stack_s = lambda i: jnp.transpose(jnp.stack([o[i] for o in outs_s], axis=0), (0, 2, 1, 3))
    return (y_prompt, y_sample, stack_p(0), stack_s(0), stack_p(1), stack_s(1),
            stack_p(2), stack_s(2), stack_p(3), stack_s(3))
```
